```python
import jax
import jax.numpy as jnp
from jax import lax
import numpy as np

D_MODEL = 4096
BATCH = 32
SEQ = 256
DEPTH = 2
DEC_BATCH = 2
DEC_SEQ = 1024
PAST_LEN = 512

GRID_W = 64
Q_BLOCK = 128
ROPE_THETA = 10000.0
NORM_EPS = 1e-6
GN_EPS = 64e-5
N_AB_LAYERS = (DEPTH + 1) // 2
N_C_LAYERS = DEPTH // 2
MLA_HEADS = D_MODEL // 256
MLA_Q_RANK = D_MODEL // 4
MLA_KV_RANK = 512
MLA_NOPE = 128
MLA_ROPE = 64
MLA_V = 128
MLA_QK = MLA_NOPE + MLA_ROPE
MLA_IN = MLA_Q_RANK + MLA_KV_RANK + MLA_ROPE
RWKV_HEAD = 64
RWKV_DIM = D_MODEL // 2
RWKV_HEADS = RWKV_DIM // RWKV_HEAD
RWKV_DECAY_RANK = 64
RWKV_A_RANK = 64
RWKV_GATE_RANK = 256
RWKV_IN = 3 * RWKV_DIM + 2 * RWKV_DECAY_RANK + 2 * RWKV_A_RANK + RWKV_GATE_RANK
AB_IN = MLA_IN + RWKV_IN
AB_OUT = MLA_HEADS * MLA_V + RWKV_DIM
ATTN_HEAD = 128
ATTN_HEADS = D_MODEL // ATTN_HEAD
ATTN_KV_HEADS = ATTN_HEADS // 4
ATTN_GROUP = ATTN_HEADS // ATTN_KV_HEADS
GQA_IN = (ATTN_HEADS + 2 * ATTN_KV_HEADS) * ATTN_HEAD
MOE_GROUPS = 8
MOE_GROUP_EXPERTS = 8
MOE_EXPERTS = MOE_GROUPS * MOE_GROUP_EXPERTS
MOE_TOPK = 2
MOE_HIDDEN = 512
MOE_BLOCK = 64

kernel_name = 'hybrid_mla_rwkv7_gqa_hmoe_diffusion_step'


def rms_norm(x, g):
    xf = x.astype(jnp.float32)
    y = xf * lax.rsqrt(jnp.mean(xf * xf, axis=-1, keepdims=True) + NORM_EPS)
    return (y * g.astype(jnp.float32)).astype(x.dtype)


def adaln_terms(cond, w, b):
    m = jax.nn.silu(cond) @ w + b
    m = m[None, None, :] if m.ndim == 1 else m[:, None, :]
    return jnp.split(m, 6, axis=-1)


def axial_rope_tables(n_tokens, rot_dim):
    rows = n_tokens // GRID_W
    row = jnp.repeat(jnp.arange(rows, dtype=jnp.float32), GRID_W)
    col = jnp.tile(jnp.arange(GRID_W, dtype=jnp.float32), rows)
    quarter = rot_dim // 4
    inv = ROPE_THETA ** (-jnp.arange(quarter, dtype=jnp.float32) / quarter)
    ang = jnp.concatenate([row[:, None] * inv, col[:, None] * inv], axis=-1)
    return jnp.cos(ang), jnp.sin(ang)


def apply_rope(x, cos, sin):
    half = x.shape[-1] // 2
    shape = (cos.shape[0],) + (1,) * (x.ndim - 3) + (half,)
    cos = cos.reshape(shape).astype(x.dtype)
    sin = sin.reshape(shape).astype(x.dtype)
    x1, x2 = x[..., :half], x[..., half:]
    return jnp.concatenate([x1 * cos - x2 * sin, x1 * sin + x2 * cos], axis=-1)


def block_attention(q, k, v):
    B, Tq, Hk, G, d = q.shape
    nb = Tq // Q_BLOCK
    qb = jnp.moveaxis(q.reshape(B, nb, Q_BLOCK, Hk, G, d), 1, 0)
    scale = d ** -0.5

    def one_block(qblk):
        s = jnp.einsum('bqhgd,bkhd->bhgqk', qblk, k).astype(jnp.float32) * scale
        p = jax.nn.softmax(s, axis=-1).astype(v.dtype)
        return jnp.einsum('bhgqk,bkhe->bqhge', p, v)

    o = lax.map(one_block, qb)
    return jnp.moveaxis(o, 0, 1).reshape(B, Tq, Hk, G, v.shape[-1])


def centred_shift(z, mu):
    prev = jnp.pad(z, ((0, 0), (1, 0), (0, 0)))[:, :-1]
    nxt = jnp.pad(z, ((0, 0), (0, 1), (0, 0)))[:, 1:]
    return z + mu[0] * (prev - z) + mu[1] * (nxt - z)


def rwkv_scan(r, decay, k, v, kk, a, s0, reverse):
    def step(S, inp):
        r_t, w_t, k_t, v_t, kk_t, a_t = inp
        sa = jnp.einsum('bhvk,bhk->bhv', S, -kk_t)
        S = (S * w_t[:, :, None, :] + sa[..., None] * (kk_t * a_t)[:, :, None, :]
             + v_t[..., None] * k_t[:, :, None, :])
        return S, jnp.einsum('bhvk,bhk->bhv', S, r_t)

    xs = tuple(jnp.moveaxis(t, 1, 0) for t in (r, decay, k, v, kk, a))
    S, ys = lax.scan(step, s0, xs, reverse=reverse)
    return jnp.moveaxis(ys, 0, 1), S


def rwkv_group(z_rwkv, p, s0=None):
    B, T, _ = z_rwkv.shape
    C, H, N = RWKV_DIM, RWKV_HEADS, RWKV_HEAD
    f32 = jnp.float32
    zr = centred_shift(z_rwkv, p['mu']).astype(f32)
    r, k, v = zr[..., :C], zr[..., C:2 * C], zr[..., 2 * C:3 * C]
    o1 = 3 * C
    o2 = o1 + 2 * RWKV_DECAY_RANK
    o3 = o2 + 2 * RWKV_A_RANK
    wd = zr[..., o1:o2].reshape(B, T, 2, RWKV_DECAY_RANK)
    ad = zr[..., o2:o3].reshape(B, T, 2, RWKV_A_RANK)
    gd = zr[..., o3:]
    w_log = -jax.nn.softplus(-(p['w0'] + jnp.einsum('btdr,drc->btdc', jnp.tanh(wd), p['w2']))) - 0.5
    decay = jnp.exp(-jnp.exp(w_log)).reshape(B, T, 2, H, N)
    a = jax.nn.sigmoid(p['a0'] + jnp.einsum('btdr,drc->btdc', ad, p['a2'])).reshape(B, T, 2, H, N)
    g = jax.nn.sigmoid(gd) @ p['g2']
    kk = (k * p['k_k']).reshape(B, T, H, N)
    kk = kk / jnp.maximum(jnp.sqrt(jnp.sum(kk * kk, axis=-1, keepdims=True)), 1e-12)
    kh = k.reshape(B, T, 1, H, N) * (1 + (a - 1) * p['k_a'].reshape(H, N))
    rh, vh = r.reshape(B, T, H, N), v.reshape(B, T, H, N)
    if s0 is None:
        s_f0 = jnp.zeros((B, H, N, N), f32)
        s_b0 = jnp.zeros((B, H, N, N), f32)
    else:
        s_f0, s_b0 = s0[0].astype(f32), s0[1].astype(f32)
    y_f, s_f = rwkv_scan(rh, decay[:, :, 0], kh[:, :, 0], vh, kk, a[:, :, 0], s_f0, False)
    y_b, s_b = rwkv_scan(rh, decay[:, :, 1], kh[:, :, 1], vh, kk, a[:, :, 1], s_b0, True)
    y = y_f + y_b
    mu = jnp.mean(y, axis=-1, keepdims=True)
    var = jnp.mean(jnp.square(y - mu), axis=-1, keepdims=True)
    yn = ((y - mu) * lax.rsqrt(var + GN_EPS)).reshape(B, T, C) * p['ln'][0] + p['ln'][1]
    bonus = jnp.sum(rh * (kh[:, :, 0] + kh[:, :, 1]) * p['r_k'], axis=-1, keepdims=True) * vh
    out = (yn + bonus.reshape(B, T, C)) * g
    dt = z_rwkv.dtype
    return out.astype(dt), s_f.astype(dt), s_b.astype(dt)


def mla_keys_values(ckv, kr, p):
    B, T, _ = ckv.shape
    kv = (ckv @ p['w_ukv']).reshape(B, T, MLA_HEADS, MLA_NOPE + MLA_V)
    k = jnp.concatenate([kv[..., :MLA_NOPE],
                         jnp.broadcast_to(kr[:, :, None, :], (B, T, MLA_HEADS, MLA_ROPE))], axis=-1)
    return rms_norm(k, p['qk_norm'][1]), kv[..., MLA_NOPE:]


def rope_tail(x, cos, sin):
    return jnp.concatenate([x[..., :MLA_NOPE], apply_rope(x[..., MLA_NOPE:], cos, sin)], axis=-1)


def ab_mixer(h, p, ctx=None, rope=None):
    B, T, _ = h.shape
    z = h @ p['w_in']
    cq = z[..., :MLA_Q_RANK]
    ckv = rms_norm(z[..., MLA_Q_RANK:MLA_Q_RANK + MLA_KV_RANK], p['kv_norm'])
    kr = z[..., MLA_Q_RANK + MLA_KV_RANK:MLA_IN]
    q = (rms_norm(cq, p['q_norm']) @ p['w_uq']).reshape(B, T, MLA_HEADS, MLA_QK)
    q = rms_norm(q, p['qk_norm'][0])
    k, v = mla_keys_values(ckv, kr, p)
    if rope is not None:
        q = rope_tail(q, rope[0], rope[1])
        k = rope_tail(k, rope[0], rope[1])
    if ctx is not None:
        k_ctx, v_ctx = mla_keys_values(ctx[0], ctx[1], p)
        k = jnp.concatenate([k_ctx, k], axis=1)
        v = jnp.concatenate([v_ctx, v], axis=1)
    o_mla = block_attention(q[:, :, :, None, :], k, v).reshape(B, T, MLA_HEADS * MLA_V)
    o_rwkv, s_f, s_b = rwkv_group(z[..., MLA_IN:], p, None if ctx is None else (ctx[2], ctx[3]))
    o = jnp.concatenate([o_mla, o_rwkv], axis=-1) @ p['w_out']
    return o, (ckv, kr, s_f, s_b)


def gqa_mixer(h, p, ctx=None, rope=None):
    B, T, _ = h.shape
    z = h @ p['w_in']
    nq = ATTN_HEADS * ATTN_HEAD
    nkv = ATTN_KV_HEADS * ATTN_HEAD
    q = rms_norm(z[..., :nq].reshape(B, T, ATTN_KV_HEADS, ATTN_GROUP, ATTN_HEAD), p['qk_norm'][0])
    k = rms_norm(z[..., nq:nq + nkv].reshape(B, T, ATTN_KV_HEADS, ATTN_HEAD), p['qk_norm'][1])
    v = z[..., nq + nkv:].reshape(B, T, ATTN_KV_HEADS, ATTN_HEAD)
    k_keep, v_keep = k, v
    if rope is not None:
        q = apply_rope(q, rope[0], rope[1])
        k = apply_rope(k, rope[0], rope[1])
    if ctx is not None:
        k = jnp.concatenate([ctx[0], k], axis=1)
        v = jnp.concatenate([ctx[1], v], axis=1)
    o = block_attention(q, k, v).reshape(B, T, nq)
    return o @ p['w_out'], (k_keep, v_keep)


def grouped_experts(x, experts, weights, w_gate, w_up, w_down):
    N, D = x.shape
    A = N * MOE_TOPK
    flat_e = experts.reshape(A)
    flat_tok = jnp.arange(A, dtype=jnp.int32) // MOE_TOPK
    flat_w = weights.reshape(A)
    order = jnp.argsort(flat_e)
    se = flat_e[order]
    counts = jnp.bincount(flat_e, length=MOE_EXPERTS)
    padded = (counts + MOE_BLOCK - 1) // MOE_BLOCK * MOE_BLOCK
    pad_end = jnp.cumsum(padded)
    pad_start = pad_end - padded
    start = jnp.cumsum(counts) - counts
    dest = pad_start[se] + jnp.arange(A, dtype=jnp.int32) - start[se]
    n_blocks = (A + MOE_EXPERTS * (MOE_BLOCK - 1) + MOE_BLOCK - 1) // MOE_BLOCK
    P = n_blocks * MOE_BLOCK
    tok_sorted = flat_tok[order]
    slot_tok = jnp.zeros((P,), jnp.int32).at[dest].set(tok_sorted)
    blk_e = jnp.minimum(jnp.searchsorted(pad_end, jnp.arange(n_blocks, dtype=jnp.int32) * MOE_BLOCK,
                                         side='right'), MOE_EXPERTS - 1)
    xb = x[slot_tok].reshape(n_blocks, MOE_BLOCK, D)

    def expert_block(args):
        xblk, e = args
        hid = jax.nn.silu(xblk @ w_gate[e]) * (xblk @ w_up[e])
        return hid @ w_down[e]

    yb = lax.map(expert_block, (xb, blk_e)).reshape(P, D)
    contrib = yb[dest] * flat_w[order][:, None]
    return jnp.zeros_like(x).at[tok_sorted].add(contrib)


def hier_moe(h, w_group, b_group, w_expert, b_expert, w_gate, w_up, w_down):
    B, T, D = h.shape
    x = h.reshape(B * T, D)
    xf = x.astype(jnp.float32)
    g_logits = xf @ w_group.astype(jnp.float32) + b_group.astype(jnp.float32)
    g_prob = jax.nn.softmax(g_logits, axis=-1)
    _, g_sel = lax.top_k(g_logits, 1)
    e_logits = (xf @ w_expert.astype(jnp.float32) + b_expert.astype(jnp.float32)).reshape(
        B * T, MOE_GROUPS, MOE_GROUP_EXPERTS)
    e_in = e_logits[jnp.arange(B * T), g_sel[:, 0]]
    e_val, e_loc = lax.top_k(e_in, MOE_TOPK)
    weights = jax.nn.softmax(e_val, axis=-1) * jnp.take_along_axis(g_prob, g_sel, axis=1)
    experts = g_sel * MOE_GROUP_EXPERTS + e_loc
    y = grouped_experts(x, experts, weights.astype(x.dtype), w_gate, w_up, w_down)
    return y.reshape(B, T, D)


def setup_inputs(seed: int = 0) -> dict:
    key = jax.random.key(seed)
    ks = iter(jax.random.split(key, 48))
    D, C, NA, NC = D_MODEL, RWKV_DIM, N_AB_LAYERS, N_C_LAYERS

    def nrm(shape, scale):
        return jax.random.normal(next(ks), shape, jnp.float32) * scale

    return {
        'x_prompt': nrm((BATCH, SEQ, D), 1.0),
        'x_sample': nrm((DEC_BATCH, DEC_SEQ, D), 1.0),
        'c': nrm((DEC_BATCH, D), 1.0),
        'c_ctx': nrm((D,), 1.0),
        'cache_mla_ckv': nrm((DEC_BATCH, NA, PAST_LEN, MLA_KV_RANK), 1.0),
        'cache_mla_krope': nrm((DEC_BATCH, NA, PAST_LEN, MLA_ROPE), 1.0),
        'state_rwkv_fwd': nrm((DEC_BATCH, NA, RWKV_HEADS, RWKV_HEAD, RWKV_HEAD), 1.0),
        'state_rwkv_bwd': nrm((DEC_BATCH, NA, RWKV_HEADS, RWKV_HEAD, RWKV_HEAD), 1.0),
        'cache_attn_k': nrm((DEC_BATCH, NC, PAST_LEN, ATTN_KV_HEADS, ATTN_HEAD), 1.0),
        'cache_attn_v': nrm((DEC_BATCH, NC, PAST_LEN, ATTN_KV_HEADS, ATTN_HEAD), 1.0),
        'mod_w': nrm((DEPTH, D, 6 * D), 0.5 * D ** -0.5),
        'mod_b': nrm((DEPTH, 6 * D), 0.01),
        'norm_mix': 1.0 + nrm((DEPTH, D), 0.05),
        'norm_ffn': 1.0 + nrm((DEPTH, D), 0.05),
        'ab_w_in': nrm((NA, D, AB_IN), D ** -0.5),
        'ab_w_out': nrm((NA, AB_OUT, D), AB_OUT ** -0.5),
        'mla_q_norm': 1.0 + nrm((NA, MLA_Q_RANK), 0.05),
        'mla_w_uq': nrm((NA, MLA_Q_RANK, MLA_HEADS * MLA_QK), MLA_Q_RANK ** -0.5),
        'mla_kv_norm': 1.0 + nrm((NA, MLA_KV_RANK), 0.05),
        'mla_w_ukv': nrm((NA, MLA_KV_RANK, MLA_HEADS * (MLA_NOPE + MLA_V)), MLA_KV_RANK ** -0.5),
        'mla_qk_norm': 1.0 + nrm((NA, 2, MLA_QK), 0.05),
        'rwkv_mu': 0.25 + nrm((NA, 2, RWKV_IN), 0.1),
        'rwkv_w0': -1.0 + nrm((NA, 2, C), 0.5),
        'rwkv_w2': nrm((NA, 2, RWKV_DECAY_RANK, C), 0.1),
        'rwkv_a0': nrm((NA, 2, C), 0.5),
        'rwkv_a2': nrm((NA, 2, RWKV_A_RANK, C), 0.1),
        'rwkv_g2': nrm((NA, RWKV_GATE_RANK, C), RWKV_GATE_RANK ** -0.5),
        'rwkv_k_k': 0.85 + nrm((NA, C), 0.05),
        'rwkv_k_a': 1.0 + nrm((NA, C), 0.05),
        'rwkv_r_k': nrm((NA, RWKV_HEADS, RWKV_HEAD), 0.1),
        'rwkv_ln': jnp.stack([1.0 + nrm((NA, C), 0.05), nrm((NA, C), 0.01)], axis=1),
        'gqa_w_in': nrm((NC, D, GQA_IN), D ** -0.5),
        'gqa_qk_norm': 1.0 + nrm((NC, 2, ATTN_HEAD), 0.05),
        'gqa_w_out': nrm((NC, ATTN_HEADS * ATTN_HEAD, D), (ATTN_HEADS * ATTN_HEAD) ** -0.5),
        'moe_w_group': nrm((DEPTH, D, MOE_GROUPS), D ** -0.5),
        'moe_b_group': nrm((DEPTH, MOE_GROUPS), 0.01),
        'moe_w_expert': nrm((DEPTH, D, MOE_EXPERTS), D ** -0.5),
        'moe_b_expert': nrm((DEPTH, MOE_EXPERTS), 0.01),
        'moe_w_gate': nrm((DEPTH, MOE_EXPERTS, D, MOE_HIDDEN), D ** -0.5),
        'moe_w_up': nrm((DEPTH, MOE_EXPERTS, D, MOE_HIDDEN), D ** -0.5),
        'moe_w_down': nrm((DEPTH, MOE_EXPERTS, MOE_HIDDEN, D), MOE_HIDDEN ** -0.5),
    }


def reference(x_prompt, x_sample, c, c_ctx, cache_mla_ckv, cache_mla_krope, state_rwkv_fwd, state_rwkv_bwd,
              cache_attn_k, cache_attn_v, mod_w, mod_b, norm_mix, norm_ffn, ab_w_in, ab_w_out, mla_q_norm,
              mla_w_uq, mla_kv_norm, mla_w_ukv, mla_qk_norm, rwkv_mu, rwkv_w0, rwkv_w2, rwkv_a0, rwkv_a2,
              rwkv_g2, rwkv_k_k, rwkv_k_a, rwkv_r_k, rwkv_ln, gqa_w_in, gqa_qk_norm, gqa_w_out, moe_w_group,
              moe_b_group, moe_w_expert, moe_b_expert, moe_w_gate, moe_w_up, moe_w_down):
    t_lat = x_sample.shape[1]
    rope_mla = axial_rope_tables(t_lat, MLA_ROPE)
    rope_gqa = axial_rope_tables(t_lat, ATTN_HEAD)
    yp, ys = x_prompt, x_sample
    ckv_l, kr_l, sf_l, sb_l, ak_l, av_l = [], [], [], [], [], []
    for layer in range(DEPTH):
        mp = adaln_terms(c_ctx, mod_w[layer], mod_b[layer])
        ms = adaln_terms(c, mod_w[layer], mod_b[layer])
        hp = rms_norm(yp, norm_mix[layer]) * (1 + mp[1]) + mp[0]
        hs = rms_norm(ys, norm_mix[layer]) * (1 + ms[1]) + ms[0]
        i = layer // 2
        if layer % 2 == 0:
            pa = dict(w_in=ab_w_in[i], w_out=ab_w_out[i], q_norm=mla_q_norm[i], w_uq=mla_w_uq[i],
                      kv_norm=mla_kv_norm[i], w_ukv=mla_w_ukv[i], qk_norm=mla_qk_norm[i], mu=rwkv_mu[i],
                      w0=rwkv_w0[i], w2=rwkv_w2[i], a0=rwkv_a0[i], a2=rwkv_a2[i], g2=rwkv_g2[i],
                      k_k=rwkv_k_k[i], k_a=rwkv_k_a[i], r_k=rwkv_r_k[i], ln=rwkv_ln[i])
            op, (ckv, kr, s_f, s_b) = ab_mixer(hp, pa)
            os_, _ = ab_mixer(hs, pa, ctx=(cache_mla_ckv[:, i], cache_mla_krope[:, i],
                                          state_rwkv_fwd[:, i], state_rwkv_bwd[:, i]), rope=rope_mla)
            ckv_l.append(ckv)
            kr_l.append(kr)
            sf_l.append(s_f)
            sb_l.append(s_b)
        else:
            pc = dict(w_in=gqa_w_in[i], qk_norm=gqa_qk_norm[i], w_out=gqa_w_out[i])
            op, (k_c, v_c) = gqa_mixer(hp, pc)
            os_, _ = gqa_mixer(hs, pc, ctx=(cache_attn_k[:, i], cache_attn_v[:, i]), rope=rope_gqa)
            ak_l.append(k_c)
            av_l.append(v_c)
        yp = yp + mp[2] * op
        ys = ys + ms[2] * os_
        moe_p = (moe_w_group[layer], moe_b_group[layer], moe_w_expert[layer], moe_b_expert[layer],
                 moe_w_gate[layer], moe_w_up[layer], moe_w_down[layer])
        yp = yp + mp[5] * hier_moe(rms_norm(yp, norm_ffn[layer]) * (1 + mp[4]) + mp[3], *moe_p)
        ys = ys + ms[5] * hier_moe(rms_norm(ys, norm_ffn[layer]) * (1 + ms[4]) + ms[3], *moe_p)
    return (yp, ys, jnp.stack(ckv_l, axis=1), jnp.stack(kr_l, axis=1), jnp.stack(sf_l, axis=1),
            jnp.stack(sb_l, axis=1), jnp.stack(ak_l, axis=1), jnp.stack(av_l, axis=1))
```

```python
import functools

import numpy as np
import jax
import jax.numpy as jnp
from jax import lax
from jax.experimental import pallas as pl
from jax.experimental.pallas import tpu as pltpu

F32 = jnp.float32
BF16 = jnp.bfloat16
I32 = jnp.int32
U32 = jnp.uint32

GRID_W = 64
ROPE_THETA = 10000.0
NORM_EPS = 1e-6
GN_EPS = 64e-5
MLA_NOPE = 128
MLA_ROPE = 64
MLA_V = 128
RWKV_HEAD = 64
ATTN_HEAD = 128
ATTN_GROUP = 4
MOE_TOPK = 2
MOE_GROUP_EXPERTS = 8

LANES = 128
RWKV_CHUNK = 64
MOE_BLOCK_ROWS = 512
VMEM_LIMIT = 56 * 1024 * 1024
NEG_BIG = -1e30


def _pick(n, prefs):
    for p in prefs:
        if n % p == 0:
            return p
    return n


def _cparams(sem):
    return pltpu.CompilerParams(dimension_semantics=sem, vmem_limit_bytes=VMEM_LIMIT)


def _dot(a, b):
    return jnp.dot(a, b, preferred_element_type=F32)


def _dot_nt(a, b):
    return lax.dot_general(a, b, (((1,), (1,)), ((), ())), preferred_element_type=F32)


def _dot_tn(a, b):
    return lax.dot_general(a, b, (((0,), (0,)), ((), ())), preferred_element_type=F32)


def _split(x):
    hi = x.astype(BF16)
    lo = (x - hi.astype(F32)).astype(BF16)
    return hi, lo


def _dot3(a, b):
    ah, al = _split(a)
    bh, bl = _split(b)
    return _dot(ah, bh) + _dot(ah, bl) + _dot(al, bh)


def _mod_kernel(c_ref, w_ref, b_ref, o_ref):
    c = c_ref[...]
    s = c * jax.nn.sigmoid(c)
    sh, sl = _split(s)
    w = w_ref[...].astype(BF16)
    o_ref[...] = _dot(sh, w) + _dot(sl, w) + b_ref[...]


def _adaln_all(cond8, mod_w, mod_b):
    depth, d, n6 = mod_w.shape
    tn = _pick(n6, (512, 256, 128))
    return pl.pallas_call(
        _mod_kernel,
        grid=(depth, n6 // tn),
        in_specs=[pl.BlockSpec((8, d), lambda l, j: (0, 0)),
                  pl.BlockSpec((None, d, tn), lambda l, j: (l, 0, j)),
                  pl.BlockSpec((None, 1, tn), lambda l, j: (l, 0, j))],
        out_specs=pl.BlockSpec((None, 8, tn), lambda l, j: (l, 0, j)),
        out_shape=jax.ShapeDtypeStruct((depth, 8, n6), F32),
        compiler_params=_cparams(("arbitrary", "arbitrary")),
    )(cond8, mod_w, mod_b.reshape(depth, 1, n6))


class _Rows:
    def __init__(self, bp, tp, bs, ts):
        self.bp, self.tp, self.bs, self.ts = bp, tp, bs, ts
        self.n_p = bp * tp
        self.n = bp * tp + bs * ts

    def group_of_block(self, tm):
        n_p, ts = self.n_p, self.ts
        assert n_p % tm == 0 and ts % tm == 0
        return lambda i: jnp.where(i * tm < n_p, 0, 1 + (i * tm - n_p) // ts)


def _norm_mod_kernel(y_ref, g_ref, sh_ref, sc_ref, o_ref):
    y = y_ref[...]
    ms = jnp.mean(y * y, axis=-1, keepdims=True)
    h = y * lax.rsqrt(ms + NORM_EPS) * g_ref[...]
    h = h * (1.0 + sc_ref[...]) + sh_ref[...]
    o_ref[...] = h.astype(o_ref.dtype)


def _norm_mod(y, gain, m6, rows, shift_idx, scale_idx):
    n, d = y.shape
    tm = _pick(rows.tp, (256, 128, 64, 32, 16, 8))
    tm = tm if rows.ts % tm == 0 else _pick(np.gcd(rows.tp, rows.ts), (256, 128, 64, 32, 16, 8))
    grp = rows.group_of_block(tm)
    return pl.pallas_call(
        _norm_mod_kernel,
        grid=(n // tm,),
        in_specs=[pl.BlockSpec((tm, d), lambda i: (i, 0)),
                  pl.BlockSpec((1, d), lambda i: (0, 0)),
                  pl.BlockSpec((None, 1, d), lambda i: (grp(i) * 6 + shift_idx, 0, 0)),
                  pl.BlockSpec((None, 1, d), lambda i: (grp(i) * 6 + scale_idx, 0, 0))],
        out_specs=pl.BlockSpec((tm, d), lambda i: (i, 0)),
        out_shape=jax.ShapeDtypeStruct((n, d), BF16),
        compiler_params=_cparams(("arbitrary",)),
    )(y, gain.reshape(1, d), m6, m6)


def _mm_kernel(*refs, n_in, has_res):
    o_ref = refs[-1]
    acc = None
    for i in range(n_in):
        w = refs[n_in + i][...]
        if w.dtype != BF16:
            w = w.astype(BF16)
        part = _dot(refs[i][...], w)
        acc = part if acc is None else acc + part
    if has_res:
        acc = refs[2 * n_in][...] + refs[2 * n_in + 1][...] * acc
    o_ref[...] = acc.astype(o_ref.dtype)


def _matmul(xs, ws, out_dtype=F32, res=None, m6=None, gate_idx=None, rows=None, tm_prefs=(1024, 512, 256, 128, 64, 32, 16, 8)):
    m = xs[0].shape[0]
    n = ws[0].shape[1]
    if res is not None:
        cands = [t for t in tm_prefs if rows.n_p % t == 0 and rows.ts % t == 0]
        tm = cands[0]
    else:
        tm = _pick(m, tm_prefs)
    tn = _pick(n, (512, 384, 256, 128))
    in_specs = [pl.BlockSpec((tm, x.shape[1]), lambda i, j: (i, 0)) for x in xs]
    in_specs += [pl.BlockSpec((w.shape[0], tn), lambda i, j: (0, j)) for w in ws]
    args = list(xs) + list(ws)
    if res is not None:
        grp = rows.group_of_block(tm)
        in_specs += [pl.BlockSpec((tm, tn), lambda i, j: (i, j)),
                     pl.BlockSpec((None, 1, tn), lambda i, j: (grp(i) * 6 + gate_idx, 0, j))]
        args += [res, m6]
    return pl.pallas_call(
        functools.partial(_mm_kernel, n_in=len(xs), has_res=res is not None),
        grid=(m // tm, n // tn),
        in_specs=in_specs,
        out_specs=pl.BlockSpec((tm, tn), lambda i, j: (i, j)),
        out_shape=jax.ShapeDtypeStruct((m, n), out_dtype),
        compiler_params=_cparams(("arbitrary", "arbitrary")),
    )(*args)


def _mla_mid_kernel(z_ref, gq_ref, gkv_ref, cq_ref, ckv_ref, ckvb_ref, *, q_rank, kv_rank):
    cq = z_ref[:, :q_rank]
    ms = jnp.mean(cq * cq, axis=-1, keepdims=True)
    cq_ref[...] = (cq * lax.rsqrt(ms + NORM_EPS) * gq_ref[...]).astype(BF16)
    ckv = z_ref[:, q_rank:q_rank + kv_rank]
    ms = jnp.mean(ckv * ckv, axis=-1, keepdims=True)
    ckvn = ckv * lax.rsqrt(ms + NORM_EPS) * gkv_ref[...]
    ckv_ref[...] = ckvn
    ckvb_ref[...] = ckvn.astype(BF16)


def _mla_mid(zm, q_norm, kv_norm, q_rank, kv_rank):
    n, w = zm.shape
    tm = _pick(n, (512, 256, 128, 64, 32, 16, 8))
    return pl.pallas_call(
        functools.partial(_mla_mid_kernel, q_rank=q_rank, kv_rank=kv_rank),
        grid=(n // tm,),
        in_specs=[pl.BlockSpec((tm, w), lambda i: (i, 0)),
                  pl.BlockSpec((1, q_rank), lambda i: (0, 0)),
                  pl.BlockSpec((1, kv_rank), lambda i: (0, 0))],
        out_specs=[pl.BlockSpec((tm, q_rank), lambda i: (i, 0)),
                   pl.BlockSpec((tm, kv_rank), lambda i: (i, 0)),
                   pl.BlockSpec((tm, kv_rank), lambda i: (i, 0))],
        out_shape=[jax.ShapeDtypeStruct((n, q_rank), BF16),
                   jax.ShapeDtypeStruct((n, kv_rank), F32),
                   jax.ShapeDtypeStruct((n, kv_rank), BF16)],
        compiler_params=_cparams(("arbitrary",)),
    )(zm, q_norm.reshape(1, q_rank), kv_norm.reshape(1, kv_rank))


def _low_half_mask(shape):
    return lax.broadcasted_iota(I32, shape, 1) < (LANES // 2)


def _mla_q_prep_kernel(z_ref, g_ref, cos_ref, sin_ref, qn_ref, qr_ref, *, heads, scale):
    hn = heads * MLA_NOPE
    hr = heads * MLA_ROPE
    qk = float(MLA_NOPE + MLA_ROPE)
    lo = _low_half_mask(cos_ref.shape)
    cos = cos_ref[...]
    sin = sin_ref[...]
    g_n, g_r, g_s = g_ref[0:1, :], g_ref[1:2, :], g_ref[2:3, :]
    for p in range(heads // 2):
        rope = z_ref[:, hn + LANES * p:hn + LANES * (p + 1)]
        rsw = z_ref[:, hn + hr + LANES * p:hn + hr + LANES * (p + 1)]
        r2 = rope * rope
        ss_r = (jnp.sum(jnp.where(lo, r2, 0.0), axis=-1, keepdims=True),
                jnp.sum(jnp.where(lo, 0.0, r2), axis=-1, keepdims=True))
        inv = []
        for hh in range(2):
            h = 2 * p + hh
            nope = z_ref[:, MLA_NOPE * h:MLA_NOPE * (h + 1)]
            ss = jnp.sum(nope * nope, axis=-1, keepdims=True) + ss_r[hh]
            inv_h = lax.rsqrt(ss / qk + NORM_EPS) * scale
            inv.append(inv_h)
            qn_ref[:, MLA_NOPE * h:MLA_NOPE * (h + 1)] = (nope * inv_h * g_n).astype(BF16)
        inv2 = jnp.where(lo, inv[0], inv[1])
        qr_ref[:, LANES * p:LANES * (p + 1)] = ((rope * g_r * cos + rsw * g_s * sin) * inv2).astype(BF16)


def _mla_q_prep(zq, g3, cos, sin, heads, scale):
    n, w = zq.shape
    tm = _pick(n, (256, 128, 64, 32, 16, 8))
    return pl.pallas_call(
        functools.partial(_mla_q_prep_kernel, heads=heads, scale=scale),
        grid=(n // tm,),
        in_specs=[pl.BlockSpec((tm, w), lambda i: (i, 0)),
                  pl.BlockSpec((8, LANES), lambda i: (0, 0)),
                  pl.BlockSpec((tm, LANES), lambda i: (i, 0)),
                  pl.BlockSpec((tm, LANES), lambda i: (i, 0))],
        out_specs=[pl.BlockSpec((tm, heads * MLA_NOPE), lambda i: (i, 0)),
                   pl.BlockSpec((tm, heads * MLA_ROPE), lambda i: (i, 0))],
        out_shape=[jax.ShapeDtypeStruct((n, heads * MLA_NOPE), BF16),
                   jax.ShapeDtypeStruct((n, heads * MLA_ROPE), BF16)],
        compiler_params=_cparams(("arbitrary",)),
    )(zq, g3, cos, sin)


def _mla_k_prep_kernel(z_ref, kr_ref, krs_ref, g_ref, cos_ref, sin_ref, kn_ref, krh_ref, v_ref, *, heads):
    hn = heads * MLA_NOPE
    qk = float(MLA_NOPE + MLA_ROPE)
    lo = _low_half_mask(cos_ref.shape)
    g_n, g_r, g_s = g_ref[0:1, :], g_ref[1:2, :], g_ref[2:3, :]
    kr = kr_ref[...]
    ss_r = jnp.sum(jnp.where(lo, kr * kr, 0.0), axis=-1, keepdims=True)
    base = kr * g_r * cos_ref[...] + krs_ref[...] * g_s * sin_ref[...]
    for p in range(heads // 2):
        inv = []
        for hh in range(2):
            h = 2 * p + hh
            nope = z_ref[:, MLA_NOPE * h:MLA_NOPE * (h + 1)]
            ss = jnp.sum(nope * nope, axis=-1, keepdims=True) + ss_r
            inv_h = lax.rsqrt(ss / qk + NORM_EPS)
            inv.append(inv_h)
            kn_ref[:, MLA_NOPE * h:MLA_NOPE * (h + 1)] = (nope * inv_h * g_n).astype(BF16)
        krh_ref[:, LANES * p:LANES * (p + 1)] = (base * jnp.where(lo, inv[0], inv[1])).astype(BF16)
    v_ref[...] = z_ref[:, hn:].astype(BF16)


def _mla_k_prep(zkv, kr2, krs2, g3, cos, sin, heads):
    n, w = zkv.shape
    tm = _pick(n, (256, 128, 64, 32, 16, 8))
    hn = heads * MLA_NOPE
    row = lambda i: (i, 0)
    return pl.pallas_call(
        functools.partial(_mla_k_prep_kernel, heads=heads),
        grid=(n // tm,),
        in_specs=[pl.BlockSpec((tm, w), row), pl.BlockSpec((tm, LANES), row), pl.BlockSpec((tm, LANES), row),
                  pl.BlockSpec((8, LANES), lambda i: (0, 0)),
                  pl.BlockSpec((tm, LANES), row), pl.BlockSpec((tm, LANES), row)],
        out_specs=[pl.BlockSpec((tm, hn), row), pl.BlockSpec((tm, heads * MLA_ROPE), row),
                   pl.BlockSpec((tm, hn), row)],
        out_shape=[jax.ShapeDtypeStruct((n, hn), BF16), jax.ShapeDtypeStruct((n, heads * MLA_ROPE), BF16),
                   jax.ShapeDtypeStruct((n, hn), BF16)],
        compiler_params=_cparams(("arbitrary",)),
    )(zkv, kr2, krs2, g3, cos, sin)


def _softmax_pv(s, v):
    m = jnp.max(s, axis=-1, keepdims=True)
    p = jnp.exp(s - m)
    l = jnp.sum(p, axis=-1, keepdims=True)
    return _dot(p.astype(BF16), v) / l


def _mla_attn_kernel(qn_ref, qr_ref, kn_ref, kr_ref, v_ref, _aliased_out, o_ref):
    lo = _low_half_mask(qr_ref.shape)
    qr = qr_ref[...]
    kr = kr_ref[...]
    zero = jnp.zeros_like(qr)
    for hh in range(2):
        sl = slice(MLA_NOPE * hh, MLA_NOPE * (hh + 1))
        qr_h = jnp.where(lo, qr, zero) if hh == 0 else jnp.where(lo, zero, qr)
        s = _dot_nt(qn_ref[:, sl], kn_ref[:, sl]) + _dot_nt(qr_h, kr)
        o_ref[:, sl] = _softmax_pv(s, v_ref[:, sl]).astype(o_ref.dtype)


def _mla_attn(qn, qr, kn, krh, v, out, heads, nb, tq_len, tk_len, q_row0, kv_row0):
    tq = _pick(tq_len, (256, 128, 64, 32, 16, 8))
    nq = tq_len // tq
    assert q_row0 % tq == 0 and kv_row0 % tk_len == 0
    qb0, kb0 = q_row0 // tq, kv_row0 // tk_len
    qmap = lambda b, p, i: (qb0 + b * nq + i, p)
    kmap = lambda b, p, i: (kb0 + b, p)
    return pl.pallas_call(
        _mla_attn_kernel,
        grid=(nb, heads // 2, nq),
        in_specs=[pl.BlockSpec((tq, 2 * MLA_NOPE), qmap), pl.BlockSpec((tq, LANES), qmap),
                  pl.BlockSpec((tk_len, 2 * MLA_NOPE), kmap), pl.BlockSpec((tk_len, LANES), kmap),
                  pl.BlockSpec((tk_len, 2 * MLA_V), kmap),
                  pl.BlockSpec(memory_space=pl.ANY)],
        out_specs=pl.BlockSpec((tq, 2 * MLA_V), qmap),
        out_shape=jax.ShapeDtypeStruct(out.shape, out.dtype),
        input_output_aliases={5: 0},
        compiler_params=_cparams(("arbitrary", "arbitrary", "arbitrary")),
    )(qn, qr, kn, krh, v, out)


def _gqa_prep_kernel(z_ref, g_ref, cos_ref, sin_ref, q_ref, kk_ref, kr_ref, v_ref, *, heads, kv_heads, scale):
    cos = cos_ref[...]
    sin = sin_ref[...]
    g_q, g_k = g_ref[0:1, :], g_ref[1:2, :]

    def normed(x, g):
        ms = jnp.mean(x * x, axis=-1, keepdims=True)
        return x * lax.rsqrt(ms + NORM_EPS) * g

    def rope(x):
        return x * cos + pltpu.roll(x, ATTN_HEAD // 2, 1) * sin

    for h in range(heads):
        sl = slice(ATTN_HEAD * h, ATTN_HEAD * (h + 1))
        q_ref[:, sl] = (rope(normed(z_ref[:, sl], g_q)) * scale).astype(BF16)
    for h in range(kv_heads):
        sl = slice(ATTN_HEAD * h, ATTN_HEAD * (h + 1))
        kn = normed(z_ref[:, ATTN_HEAD * (heads + h):ATTN_HEAD * (heads + h + 1)], g_k)
        kk_ref[:, sl] = kn
        kr_ref[:, sl] = rope(kn).astype(BF16)
    v_ref[...] = z_ref[:, ATTN_HEAD * (heads + kv_heads):].astype(BF16)


def _gqa_prep(z, g2, cos, sin, heads, kv_heads, scale):
    n, w = z.shape
    tm = _pick(n, (256, 128, 64, 32, 16, 8))
    row = lambda i: (i, 0)
    nq, nk = heads * ATTN_HEAD, kv_heads * ATTN_HEAD
    return pl.pallas_call(
        functools.partial(_gqa_prep_kernel, heads=heads, kv_heads=kv_heads, scale=scale),
        grid=(n // tm,),
        in_specs=[pl.BlockSpec((tm, w), row), pl.BlockSpec((8, LANES), lambda i: (0, 0)),
                  pl.BlockSpec((tm, LANES), row), pl.BlockSpec((tm, LANES), row)],
        out_specs=[pl.BlockSpec((tm, nq), row), pl.BlockSpec((tm, nk), row),
                   pl.BlockSpec((tm, nk), row), pl.BlockSpec((tm, nk), row)],
        out_shape=[jax.ShapeDtypeStruct((n, nq), BF16), jax.ShapeDtypeStruct((n, nk), F32),
                   jax.ShapeDtypeStruct((n, nk), BF16), jax.ShapeDtypeStruct((n, nk), BF16)],
        compiler_params=_cparams(("arbitrary",)),
    )(z, g2, cos, sin)


def _gqa_attn_kernel(q_ref, k_ref, v_ref, _aliased_out, o_ref):
    k = k_ref[...]
    v = v_ref[...]
    for g in range(ATTN_GROUP):
        sl = slice(ATTN_HEAD * g, ATTN_HEAD * (g + 1))
        o_ref[:, sl] = _softmax_pv(_dot_nt(q_ref[:, sl], k), v).astype(o_ref.dtype)


def _gqa_attn(q, k, v, out, kv_heads, nb, tq_len, tk_len, q_row0, kv_row0):
    tq = _pick(tq_len, (256, 128, 64, 32, 16, 8))
    nq = tq_len // tq
    assert q_row0 % tq == 0 and kv_row0 % tk_len == 0
    qb0, kb0 = q_row0 // tq, kv_row0 // tk_len
    qmap = lambda b, h, i: (qb0 + b * nq + i, h)
    kmap = lambda b, h, i: (kb0 + b, h)
    gw = ATTN_GROUP * ATTN_HEAD
    return pl.pallas_call(
        _gqa_attn_kernel,
        grid=(nb, kv_heads, nq),
        in_specs=[pl.BlockSpec((tq, gw), qmap), pl.BlockSpec((tk_len, ATTN_HEAD), kmap),
                  pl.BlockSpec((tk_len, ATTN_HEAD), kmap), pl.BlockSpec(memory_space=pl.ANY)],
        out_specs=pl.BlockSpec((tq, gw), qmap),
        out_shape=jax.ShapeDtypeStruct(out.shape, out.dtype),
        input_output_aliases={3: 0},
        compiler_params=_cparams(("arbitrary", "arbitrary", "arbitrary")),
    )(q, k, v, out)


def _seg_sum(x, lo):
    s0 = jnp.sum(jnp.where(lo, x, 0.0), axis=-1, keepdims=True)
    s1 = jnp.sum(jnp.where(lo, 0.0, x), axis=-1, keepdims=True)
    return jnp.where(lo, s0, s1)


def _centred_shift(z, mu_ref):
    t = z.shape[0]
    ti = lax.broadcasted_iota(I32, z.shape, 0)
    prev = jnp.where(ti == 0, 0.0, pltpu.roll(z, 1, 0))
    nxt = jnp.where(ti == t - 1, 0.0, pltpu.roll(z, t - 1, 0))
    return z + mu_ref[0:1, :] * (prev - z) + mu_ref[1:2, :] * (nxt - z)


def _stack_heads(x, lo):
    return jnp.concatenate([jnp.where(lo, x, 0.0), jnp.where(lo, 0.0, x)], axis=0)


def _rwkv_chunk(d, c0, st, r_s, v_s, kk_s, lw_s, kh_s, bb_s, consts):
    L = RWKV_CHUNK
    tri, strict, incl, eye, lo = consts[d]
    rows = pl.ds(c0, L)
    lw = lw_s[d, rows, :]
    hi = lw.astype(BF16)
    r1 = lw - hi.astype(F32)
    mid = r1.astype(BF16)
    low = (r1 - mid.astype(F32)).astype(BF16)
    cum = _dot(tri, hi) + _dot(tri, mid) + _dot(tri, low)
    g_inc = jnp.exp(cum)
    g_inv = jnp.exp(-cum)
    kap = _stack_heads(kk_s[rows, :] * jnp.exp(cum - lw), lo).astype(BF16)
    g_end = g_inc[L - 1:L, :] if d == 0 else g_inc[0:1, :]
    bt32 = _stack_heads(bb_s[d, rows, :] * g_inv, lo)
    kt32 = _stack_heads(kh_s[d, rows, :] * g_inv, lo)
    rt32 = _stack_heads(r_s[rows, :] * g_inc, lo)
    bt, kt, rt = bt32.astype(BF16), kt32.astype(BF16), rt32.astype(BF16)
    vr = _stack_heads(v_s[rows, :], lo).astype(BF16)
    lb = jnp.where(strict, _dot_nt(kap, bt), 0.0)
    lk = jnp.where(strict, _dot_nt(kap, kt), 0.0).astype(BF16)
    hb = jnp.where(incl, _dot_nt(rt, bt), 0.0).astype(BF16)
    hk = jnp.where(incl, _dot_nt(rt, kt), 0.0).astype(BF16)
    x = jnp.where(eye, 1.0, 0.0) - lb
    lbb = lb.astype(BF16)
    p = _dot(lbb, lbb)
    n_iter = int(np.log2(L)) - 1
    for it in range(n_iter):
        pb = p.astype(BF16)
        x = x + _dot(x.astype(BF16), pb)
        if it + 1 < n_iter:
            p = _dot(pb, pb)
    xb = x.astype(BF16)
    a1 = (-_dot(xb, kap)).astype(BF16)
    a2 = (-_dot(xb, _dot(lk, vr).astype(BF16))).astype(BF16)
    btg = (bt32 * g_end).astype(BF16)
    ktg = (kt32 * g_end).astype(BF16)
    m = jnp.where(eye, jnp.broadcast_to(g_end, eye.shape), 0.0) + _dot_tn(btg, a1)
    nn = _dot_tn(btg, a2) + _dot_tn(ktg, vr)
    y1 = rt32 + _dot(hb, a1)
    y2 = _dot(hb, a2) + _dot(hk, vr)
    stb = st.astype(BF16)
    ysm = _dot(y1.astype(BF16), stb) + y2
    st_new = _dot(m.astype(BF16), stb) + nn
    return ysm[:L, :] + ysm[L:, :], st_new


def _rwkv_kernel(*refs, t_len, has_s0):
    (zr_ref, zk_ref, zv_ref, zs_ref, mur_ref, muk_ref, muv_ref, mus_ref, w0_ref, w2_ref, a0_ref, a2_ref,
     g2_ref, kkg_ref, ka_ref, rk_ref, ln_ref) = refs[:17]
    pos = 17
    if has_s0:
        s0f_ref, s0b_ref = refs[17:19]
        pos = 19
    o_ref, sf_ref, sb_ref = refs[pos:pos + 3]
    r_s, v_s, kk_s, bonus_s, g_s, lw_s, kh_s, bb_s, y_s = refs[pos + 3:]
    L = RWKV_CHUNK
    n_chunks = t_len // L
    lo = _low_half_mask((t_len, LANES))

    r = _centred_shift(zr_ref[...], mur_ref)
    k = _centred_shift(zk_ref[...], muk_ref)
    v = _centred_shift(zv_ref[...], muv_ref)
    s = _centred_shift(zs_ref[...], mus_ref)
    tw = jnp.tanh(s[:, 0:LANES])
    ad = s[:, LANES:2 * LANES]
    g_s[...] = _dot3(jax.nn.sigmoid(s[:, 2 * LANES:]), g2_ref[...])
    kk = k * kkg_ref[...]
    kk = kk / jnp.maximum(jnp.sqrt(_seg_sum(kk * kk, lo)), 1e-12)
    kh_sum = jnp.zeros_like(k)
    for d in range(2):
        keep = lo if d == 0 else jnp.logical_not(lo)
        w_pre = w0_ref[d:d + 1, :] + _dot3(jnp.where(keep, tw, 0.0), w2_ref[...])
        neg = -w_pre
        softplus = jnp.maximum(neg, 0.0) + jnp.log1p(jnp.exp(-jnp.abs(neg)))
        lw_s[d] = -jnp.exp(-softplus - 0.5)
        a = jax.nn.sigmoid(a0_ref[d:d + 1, :] + _dot3(jnp.where(keep, ad, 0.0), a2_ref[...]))
        kh = k * (1.0 + (a - 1.0) * ka_ref[...])
        kh_s[d] = kh
        bb_s[d] = kk * a
        kh_sum = kh_sum + kh
    bonus_s[...] = _seg_sum(r * kh_sum * rk_ref[...], lo) * v
    r_s[...] = r
    v_s[...] = v
    kk_s[...] = kk

    ri = lax.broadcasted_iota(I32, (2 * L, 2 * L), 0)
    ci = lax.broadcasted_iota(I32, (2 * L, 2 * L), 1)
    tt, ss = ri & (L - 1), ci & (L - 1)
    eye = ri == ci
    ti = lax.broadcasted_iota(I32, (L, L), 0)
    si = lax.broadcasted_iota(I32, (L, L), 1)
    lo_c = _low_half_mask((L, LANES))
    consts = ((jnp.where(si <= ti, 1.0, 0.0).astype(BF16), tt > ss, tt >= ss, eye, lo_c),
              (jnp.where(si >= ti, 1.0, 0.0).astype(BF16), tt < ss, tt <= ss, eye, lo_c))

    if has_s0:
        sf_ref[...] = s0f_ref[...]
        sb_ref[...] = s0b_ref[...]
    else:
        sf_ref[...] = jnp.zeros((2 * L, 2 * L), F32)
        sb_ref[...] = jnp.zeros((2 * L, 2 * L), F32)

    def body(i, carry):
        cf = pl.multiple_of(i * L, L)
        cb = pl.multiple_of((n_chunks - 1 - i) * L, L)
        y_f, st_f = _rwkv_chunk(0, cf, sf_ref[...], r_s, v_s, kk_s, lw_s, kh_s, bb_s, consts)
        y_s[0, pl.ds(cf, L), :] = y_f
        sf_ref[...] = st_f
        y_b, st_b = _rwkv_chunk(1, cb, sb_ref[...], r_s, v_s, kk_s, lw_s, kh_s, bb_s, consts)
        y_s[1, pl.ds(cb, L), :] = y_b
        sb_ref[...] = st_b
        return carry

    lax.fori_loop(0, n_chunks, body, 0)

    y = y_s[0] + y_s[1]
    inv_n = 1.0 / RWKV_HEAD
    mu = _seg_sum(y, lo) * inv_n
    yc = y - mu
    var = _seg_sum(yc * yc, lo) * inv_n
    yn = yc * lax.rsqrt(var + GN_EPS) * ln_ref[0:1, :] + ln_ref[1:2, :]
    o_ref[...] = ((yn + bonus_s[...]) * g_s[...]).astype(o_ref.dtype)


def _rwkv(z, out, par, nb, t_len, row0, s0=None):
    c = par['w0'].shape[1]
    pairs = c // LANES
    assert t_len % RWKV_CHUNK == 0 and row0 % t_len == 0 and (3 * c) % (4 * LANES) == 0
    rb0 = row0 // t_len
    cb = c // LANES
    sb = 3 * c // (4 * LANES)
    zmap = lambda off: (lambda b, j: (rb0 + b, off + j))
    cmap = lambda off: (lambda b, j: (0, off + j))
    in_specs = [pl.BlockSpec((t_len, LANES), zmap(0)), pl.BlockSpec((t_len, LANES), zmap(cb)),
                pl.BlockSpec((t_len, LANES), zmap(2 * cb)),
                pl.BlockSpec((t_len, 4 * LANES), lambda b, j: (rb0 + b, sb)),
                pl.BlockSpec((2, LANES), cmap(0)), pl.BlockSpec((2, LANES), cmap(cb)),
                pl.BlockSpec((2, LANES), cmap(2 * cb)), pl.BlockSpec((2, 4 * LANES), lambda b, j: (0, sb)),
                pl.BlockSpec((2, LANES), cmap(0)), pl.BlockSpec((LANES, LANES), cmap(0)),
                pl.BlockSpec((2, LANES), cmap(0)), pl.BlockSpec((LANES, LANES), cmap(0)),
                pl.BlockSpec((2 * LANES, LANES), cmap(0)),
                pl.BlockSpec((1, LANES), cmap(0)), pl.BlockSpec((1, LANES), cmap(0)),
                pl.BlockSpec((1, LANES), cmap(0)), pl.BlockSpec((2, LANES), cmap(0))]
    args = [z, z, z, z, par['mu'], par['mu'], par['mu'], par['mu'], par['w0'], par['w2'], par['a0'], par['a2'],
            par['g2'], par['k_k'], par['k_a'], par['r_k'], par['ln']]
    smap = lambda b, j: (b, j, 0, 0)
    if s0 is not None:
        in_specs += [pl.BlockSpec((None, None, LANES, LANES), smap)] * 2
        args += [s0[0], s0[1]]
    in_specs.append(pl.BlockSpec(memory_space=pl.ANY))
    args.append(out)
    st_shape = jax.ShapeDtypeStruct((nb, pairs, LANES, LANES), F32)
    tl = pltpu.VMEM((t_len, LANES), F32)
    tl2 = pltpu.VMEM((2, t_len, LANES), F32)

    def kern(*refs):
        n_in = len(args)
        _rwkv_kernel(*refs[:n_in - 1], *refs[n_in:], t_len=t_len, has_s0=s0 is not None)

    return pl.pallas_call(
        kern,
        grid=(nb, pairs),
        in_specs=in_specs,
        out_specs=[pl.BlockSpec((t_len, LANES), zmap(0)),
                   pl.BlockSpec((None, None, LANES, LANES), smap),
                   pl.BlockSpec((None, None, LANES, LANES), smap)],
        out_shape=[jax.ShapeDtypeStruct(out.shape, out.dtype), st_shape, st_shape],
        scratch_shapes=[tl, tl, tl, tl, tl, tl2, tl2, tl2, tl2],
        input_output_aliases={len(args) - 1: 0},
        compiler_params=_cparams(("arbitrary", "arbitrary")),
    )(*args)


def _rwkv_params(p):
    c = p['w0'].shape[1]
    return dict(mu=p['mu'], w0=p['w0'], w2=p['w2'].reshape(-1, c), a0=p['a0'], a2=p['a2'].reshape(-1, c),
                g2=p['g2'], k_k=p['k_k'].reshape(1, c), k_a=p['k_a'].reshape(1, c), r_k=p['r_k'].reshape(1, c),
                ln=p['ln'])


def _states_to_blockdiag(s):
    b, h, n, _ = s.shape
    st = jnp.swapaxes(s, 2, 3).reshape(b, h // 2, 2, n, n)
    z = jnp.zeros((b, h // 2, n, n), s.dtype)
    top = jnp.concatenate([st[:, :, 0], z], axis=-1)
    bot = jnp.concatenate([z, st[:, :, 1]], axis=-1)
    return jnp.concatenate([top, bot], axis=-2)


def _blockdiag_to_states(bd):
    b, hp, _, _ = bd.shape
    n = RWKV_HEAD
    st = jnp.stack([bd[:, :, :n, :n], bd[:, :, n:, n:]], axis=2)
    return jnp.swapaxes(st, 3, 4).reshape(b, 2 * hp, n, n)


_HI_MASK = 0xFFFF0000


def _pack_pairs(x):
    w = x.shape[1] // 2
    bits = lax.bitcast_convert_type(x.astype(BF16).astype(F32), U32)
    return (bits[:, :w] >> 16) | (bits[:, w:] & jnp.uint32(_HI_MASK))


def _unpack_pairs(p):
    lo = lax.bitcast_convert_type(p << 16, F32)
    hi = lax.bitcast_convert_type(p & jnp.uint32(_HI_MASK), F32)
    return lo, hi


def _ffn_pre_kernel(y_ref, g_ref, sh_ref, sc_ref, wh_ref, wl_ref, b_ref, hp_ref, route_ref, *, n_exp, n_grp):
    y = y_ref[...]
    ms = jnp.mean(y * y, axis=-1, keepdims=True)
    h = y * lax.rsqrt(ms + NORM_EPS) * g_ref[...]
    h = h * (1.0 + sc_ref[...]) + sh_ref[...]
    hp_ref[...] = _pack_pairs(h)
    hh, hl = _split(h)
    wh = wh_ref[...]
    lg = _dot(hh, wh) + _dot(hl, wh) + _dot(hh, wl_ref[...]) + b_ref[...]
    lane = lax.broadcasted_iota(I32, lg.shape, 1)
    lane_f = lane.astype(F32)
    big = 1e9
    is_g = (lane >= n_exp) & (lane < n_exp + n_grp)
    gl = jnp.where(is_g, lg, NEG_BIG)
    gmax = jnp.max(gl, axis=-1, keepdims=True)
    gidx = jnp.min(jnp.where(is_g & (gl == gmax), lane_f, big), axis=-1, keepdims=True) - n_exp
    gprob = 1.0 / jnp.sum(jnp.where(is_g, jnp.exp(gl - gmax), 0.0), axis=-1, keepdims=True)
    gstart = gidx * MOE_GROUP_EXPERTS
    in_grp = (lane_f >= gstart) & (lane_f < gstart + MOE_GROUP_EXPERTS)
    el = jnp.where(in_grp, lg, NEG_BIG)
    e1v = jnp.max(el, axis=-1, keepdims=True)
    e1i = jnp.min(jnp.where(in_grp & (el == e1v), lane_f, big), axis=-1, keepdims=True)
    rest = in_grp & (lane_f != e1i)
    el2 = jnp.where(rest, lg, NEG_BIG)
    e2v = jnp.max(el2, axis=-1, keepdims=True)
    e2i = jnp.min(jnp.where(rest & (el2 == e2v), lane_f, big), axis=-1, keepdims=True)
    t = jnp.exp(e2v - e1v)
    den = 1.0 + t
    w1 = gprob / den
    w2 = gprob * t / den
    route_ref[...] = jnp.where(lane == 0, e1i, jnp.where(lane == 1, e2i,
                               jnp.where(lane == 2, w1, jnp.where(lane == 3, w2, 0.0))))


def _ffn_pre(y, gain, m6, rows, wr_hi, wr_lo, br, n_exp, n_grp):
    n, d = y.shape
    tm = _pick(np.gcd(rows.n_p, rows.ts), (256, 128, 64, 32, 16, 8))
    grp = rows.group_of_block(tm)
    const = lambda i: (0, 0)
    return pl.pallas_call(
        functools.partial(_ffn_pre_kernel, n_exp=n_exp, n_grp=n_grp),
        grid=(n // tm,),
        in_specs=[pl.BlockSpec((tm, d), lambda i: (i, 0)),
                  pl.BlockSpec((1, d), const),
                  pl.BlockSpec((None, 1, d), lambda i: (grp(i) * 6 + 3, 0, 0)),
                  pl.BlockSpec((None, 1, d), lambda i: (grp(i) * 6 + 4, 0, 0)),
                  pl.BlockSpec((d, LANES), const), pl.BlockSpec((d, LANES), const),
                  pl.BlockSpec((1, LANES), const)],
        out_specs=[pl.BlockSpec((tm, d // 2), lambda i: (i, 0)), pl.BlockSpec((tm, LANES), lambda i: (i, 0))],
        out_shape=[jax.ShapeDtypeStruct((n, d // 2), U32), jax.ShapeDtypeStruct((n, LANES), F32)],
        compiler_params=_cparams(("arbitrary",)),
    )(y, gain.reshape(1, d), m6, m6, wr_hi, wr_lo, br)


def _dispatch_kernel(dest_ref, hp_ref, _zeros_ref, xs_ref, sem, *, chunk):
    base = pl.program_id(0) * chunk

    def row_copy(src_row, dst_row):
        return pltpu.make_async_copy(hp_ref.at[pl.ds(src_row, 1), :], xs_ref.at[pl.ds(dst_row, 1), :], sem)

    def start(a, carry):
        aa = base + a
        row_copy(lax.shift_right_logical(aa, 1), dest_ref[aa]).start()
        return carry

    def wait(a, carry):
        row_copy(0, 0).wait()
        return carry

    lax.fori_loop(0, chunk, start, 0)
    lax.fori_loop(0, chunk, wait, 0)


def _moe_dispatch(dest, hp, n_slots):
    n_assign = dest.shape[0]
    chunk = _pick(n_assign, (1024, 512, 256, 128, 64, 32, 16, 8))
    any_spec = pl.BlockSpec(memory_space=pl.ANY)
    return pl.pallas_call(
        functools.partial(_dispatch_kernel, chunk=chunk),
        grid_spec=pltpu.PrefetchScalarGridSpec(
            num_scalar_prefetch=1, grid=(n_assign // chunk,),
            in_specs=[any_spec, any_spec], out_specs=any_spec,
            scratch_shapes=[pltpu.SemaphoreType.DMA]),
        out_shape=jax.ShapeDtypeStruct((n_slots, hp.shape[1]), hp.dtype),
        input_output_aliases={2: 0},
        compiler_params=_cparams(("arbitrary",)),
    )(dest, hp, jnp.zeros((n_slots, hp.shape[1]), hp.dtype))


def _expert_kernel(blk_e_ref, nact_ref, x_ref, wg_ref, wu_ref, wd_ref, o_ref, acc_ref, *, n_hid_blocks):
    i = pl.program_id(0)
    j = pl.program_id(1)

    @pl.when(i < nact_ref[0])
    def _():
        x_lo, x_hi = _unpack_pairs(x_ref[...])
        x_lo, x_hi = x_lo.astype(BF16), x_hi.astype(BF16)
        half = x_ref.shape[1]

        def proj(w_ref):
            return _dot(x_lo, w_ref[:half, :].astype(BF16)) + _dot(x_hi, w_ref[half:, :].astype(BF16))

        gate = proj(wg_ref)
        hid = gate * jax.nn.sigmoid(gate) * proj(wu_ref)
        part = _dot(hid.astype(BF16), wd_ref[...].astype(BF16))

        @pl.when(j == 0)
        def _():
            acc_ref[...] = part

        @pl.when(j > 0)
        def _():
            acc_ref[...] += part

        @pl.when(j == n_hid_blocks - 1)
        def _():
            o_ref[...] = _pack_pairs(acc_ref[...])


def _moe_experts(xs, blk_e, nact, w_gate, w_up, w_down):
    n_slots, half = xs.shape
    n_exp, d, hid = w_gate.shape
    bm = MOE_BLOCK_ROWS
    hb = _pick(hid, (256, 128))
    nh = hid // hb

    def active(i, j, blk_e, nact):
        ib = jnp.minimum(i, nact[0] - 1)
        jb = jnp.where(i < nact[0], j, nh - 1)
        return ib, jb

    def xmap(i, j, blk_e, nact):
        return active(i, j, blk_e, nact)[0], 0

    def gmap(i, j, blk_e, nact):
        ib, jb = active(i, j, blk_e, nact)
        return blk_e[ib], 0, jb

    def dmap(i, j, blk_e, nact):
        ib, jb = active(i, j, blk_e, nact)
        return blk_e[ib], jb, 0

    return pl.pallas_call(
        functools.partial(_expert_kernel, n_hid_blocks=nh),
        grid_spec=pltpu.PrefetchScalarGridSpec(
            num_scalar_prefetch=2, grid=(n_slots // bm, nh),
            in_specs=[pl.BlockSpec((bm, half), xmap),
                      pl.BlockSpec((None, d, hb), gmap), pl.BlockSpec((None, d, hb), gmap),
                      pl.BlockSpec((None, hb, d), dmap)],
            out_specs=pl.BlockSpec((bm, half), xmap),
            scratch_shapes=[pltpu.VMEM((bm, d), F32)]),
        out_shape=jax.ShapeDtypeStruct((n_slots, half), U32),
        input_output_aliases={2: 0},
        compiler_params=_cparams(("arbitrary", "arbitrary")),
    )(blk_e, nact, xs, w_gate, w_up, w_down)


def _combine_kernel(dest_ref, y_ref, gate_ref, w_ref, ys_ref, o_ref, buf, sem, *, tm, n_steps):
    i = pl.program_id(0)
    slot = lax.rem(i, 2)

    def row_copy(src_row, s, k, r):
        return pltpu.make_async_copy(ys_ref.at[pl.ds(src_row, 1), :], buf.at[s, k, pl.ds(r, 1), :], sem.at[s])

    def issue(step, s):
        def body(r, carry):
            a = 2 * (step * tm + r)
            row_copy(dest_ref[a], s, 0, r).start()
            row_copy(dest_ref[a + 1], s, 1, r).start()
            return carry
        lax.fori_loop(0, tm, body, 0)

    @pl.when(i == 0)
    def _():
        issue(0, 0)

    @pl.when(i + 1 < n_steps)
    def _():
        issue(i + 1, 1 - slot)

    def wait(r, carry):
        row_copy(0, slot, 0, 0).wait()
        row_copy(0, slot, 1, 0).wait()
        return carry
    lax.fori_loop(0, tm, wait, 0)

    half = buf.shape[-1]
    lo0, hi0 = _unpack_pairs(buf[slot, 0])
    lo1, hi1 = _unpack_pairs(buf[slot, 1])
    w0 = w_ref[:, 0:1]
    w1 = w_ref[:, 1:2]
    o_ref[:, :half] = y_ref[:, :half] + gate_ref[:, :half] * (w0 * lo0 + w1 * lo1)
    o_ref[:, half:] = y_ref[:, half:] + gate_ref[:, half:] * (w0 * hi0 + w1 * hi1)


def _moe_combine(dest, y, m6, rows, wts, ys):
    n, d = y.shape
    tm = _pick(np.gcd(rows.n_p, rows.ts), (128, 64, 32, 16, 8))
    grp = rows.group_of_block(tm)
    n_steps = n // tm
    return pl.pallas_call(
        functools.partial(_combine_kernel, tm=tm, n_steps=n_steps),
        grid_spec=pltpu.PrefetchScalarGridSpec(
            num_scalar_prefetch=1, grid=(n_steps,),
            in_specs=[pl.BlockSpec((tm, d), lambda i, dest: (i, 0)),
                      pl.BlockSpec((None, 1, d), lambda i, dest: (grp(i) * 6 + 5, 0, 0)),
                      pl.BlockSpec((tm, MOE_TOPK), lambda i, dest: (i, 0)),
                      pl.BlockSpec(memory_space=pl.ANY)],
            out_specs=pl.BlockSpec((tm, d), lambda i, dest: (i, 0)),
            scratch_shapes=[pltpu.VMEM((2, MOE_TOPK, tm, d // 2), U32), pltpu.SemaphoreType.DMA((2,))]),
        out_shape=jax.ShapeDtypeStruct((n, d), F32),
        compiler_params=_cparams(("arbitrary",)),
    )(dest, y, m6, wts, ys)


def _moe_layer(y, gain, m6, rows, w_group, b_group, w_expert, b_expert, w_gate, w_up, w_down):
    n, d = y.shape
    n_exp = w_expert.shape[1]
    n_grp = w_group.shape[1]
    assert n_exp + n_grp <= LANES and n_exp == n_grp * MOE_GROUP_EXPERTS
    pad = jnp.zeros((d, LANES - n_exp - n_grp), F32)
    wr = jnp.concatenate([w_expert, w_group, pad], axis=1)
    wr_hi, wr_lo = _split(wr)
    br = jnp.concatenate([b_expert, b_group, jnp.zeros((LANES - n_exp - n_grp,), F32)]).reshape(1, LANES)
    hp, route = _ffn_pre(y, gain, m6, rows, wr_hi, wr_lo, br, n_exp, n_grp)

    bm = MOE_BLOCK_ROWS
    n_assign = n * MOE_TOPK
    flat_e = route[:, :MOE_TOPK].astype(I32).reshape(n_assign)
    wts = route[:, MOE_TOPK:2 * MOE_TOPK]
    onehot = (flat_e[:, None] == jnp.arange(n_exp, dtype=I32)[None, :]).astype(I32)
    csum = jnp.cumsum(onehot, axis=0)
    pos = jnp.take_along_axis(csum, flat_e[:, None], axis=1)[:, 0] - 1
    counts = csum[-1]
    padded = (counts + bm - 1) // bm * bm
    pad_end = jnp.cumsum(padded)
    dest = (pad_end - padded)[flat_e] + pos
    n_blocks = (n_assign + n_exp * (bm - 1) + bm - 1) // bm
    blk_e = jnp.minimum(jnp.searchsorted(pad_end, jnp.arange(n_blocks, dtype=I32) * bm, side='right'),
                        n_exp - 1).astype(I32)
    nact = (pad_end[-1:] // bm).astype(I32)

    xs = _moe_dispatch(dest, hp, n_blocks * bm)
    ys = _moe_experts(xs, blk_e, nact, w_gate, w_up, w_down)
    return _moe_combine(dest, y, m6, rows, wts, ys)


def _rope_rows(t_len, rot_dim):
    rows = t_len // GRID_W
    row = jnp.repeat(jnp.arange(rows, dtype=F32), GRID_W)
    col = jnp.tile(jnp.arange(GRID_W, dtype=F32), rows)
    quarter = rot_dim // 4
    inv = ROPE_THETA ** (-jnp.arange(quarter, dtype=F32) / quarter)
    ang = jnp.concatenate([row[:, None] * inv, col[:, None] * inv], axis=-1)
    cos, sin = jnp.cos(ang), jnp.sin(ang)
    reps = LANES // rot_dim
    return (jnp.tile(jnp.concatenate([cos, cos], axis=-1), (1, reps)),
            jnp.tile(jnp.concatenate([-sin, sin], axis=-1), (1, reps)))


def _token_tables(rows, rot_dim):
    cos_s, sin_s = _rope_rows(rows.ts, rot_dim)
    cos = jnp.concatenate([jnp.ones((rows.n_p, LANES), F32)] + [cos_s] * rows.bs, axis=0)
    sin = jnp.concatenate([jnp.zeros((rows.n_p, LANES), F32)] + [sin_s] * rows.bs, axis=0)
    return cos, sin


def _kv_order(rows, tok, ctx):
    parts = []
    for b in range(rows.bs):
        parts += [ctx[b], tok[rows.n_p + b * rows.ts:rows.n_p + (b + 1) * rows.ts]]
    parts.append(tok[:rows.n_p])
    return jnp.concatenate(parts, axis=0)


def _pad_rows8(rows_list):
    a = jnp.stack(rows_list, axis=0)
    return jnp.concatenate([a, jnp.zeros((8 - a.shape[0], a.shape[1]), a.dtype)], axis=0)


def _ab_layer(y, h, m6, rows, past, p, ctx):
    n, d = y.shape
    q_rank = p['q_norm'].shape[0]
    kv_rank = p['kv_norm'].shape[0]
    heads = p['w_uq'].shape[1] // (MLA_NOPE + MLA_ROPE)
    mla_in = q_rank + kv_rank + MLA_ROPE
    half = MLA_ROPE // 2
    swap = np.concatenate([np.arange(half, MLA_ROPE), np.arange(0, half)])

    w_in = p['w_in']
    w_kr = w_in[:, q_rank + kv_rank:mla_in]
    w_mla = jnp.concatenate([w_in[:, :q_rank + kv_rank], w_kr, w_kr, w_kr[:, swap], w_kr[:, swap]], axis=1).astype(BF16)
    zm = _matmul([h], [w_mla])
    z_rwkv = _matmul([h], [w_in[:, mla_in:].astype(BF16)])

    cqn, ckvn, ckvn_b = _mla_mid(zm, p['q_norm'], p['kv_norm'], q_rank, kv_rank)
    w_uq = p['w_uq'].reshape(q_rank, heads, MLA_NOPE + MLA_ROPE)
    w_rope = w_uq[:, :, MLA_NOPE:]
    w_q = jnp.concatenate([w_uq[:, :, :MLA_NOPE].reshape(q_rank, -1), w_rope.reshape(q_rank, -1),
                           w_rope[:, :, swap].reshape(q_rank, -1)], axis=1).astype(BF16)
    zq = _matmul([cqn], [w_q])
    cos_t, sin_t = _token_tables(rows, MLA_ROPE)

    def gains(g):
        g_r = g[MLA_NOPE:]
        return _pad_rows8([g[:MLA_NOPE], jnp.concatenate([g_r, g_r]), jnp.concatenate([g_r[swap], g_r[swap]])])

    scale = float(MLA_NOPE + MLA_ROPE) ** -0.5
    qn, qr = _mla_q_prep(zq, gains(p['qk_norm'][0]), cos_t, sin_t, heads, scale)

    ctx_ckv, ctx_kr, s_f0, s_b0 = ctx
    w_ukv = p['w_ukv'].reshape(kv_rank, heads, MLA_NOPE + MLA_V)
    w_kv = jnp.concatenate([w_ukv[:, :, :MLA_NOPE].reshape(kv_rank, -1), w_ukv[:, :, MLA_NOPE:].reshape(kv_rank, -1)],
                           axis=1).astype(BF16)
    zkv = _matmul([_kv_order(rows, ckvn_b, ctx_ckv.astype(BF16))], [w_kv])
    kr_off = q_rank + kv_rank
    ctx_kr2 = jnp.concatenate([ctx_kr, ctx_kr], axis=-1)
    kr2 = _kv_order(rows, zm[:, kr_off:kr_off + LANES], ctx_kr2)
    krs2 = _kv_order(rows, zm[:, kr_off + LANES:kr_off + 2 * LANES], ctx_kr2)
    one_c = jnp.ones((rows.bs, past, LANES), F32)
    cos_kv = _kv_order(rows, cos_t, one_c)
    sin_kv = _kv_order(rows, sin_t, 0.0 * one_c)
    kn, krh, vv = _mla_k_prep(zkv, kr2, krs2, gains(p['qk_norm'][1]), cos_kv, sin_kv, heads)

    o_mla = jnp.zeros((n, heads * MLA_V), BF16)
    o_mla = _mla_attn(qn, qr, kn, krh, vv, o_mla, heads, rows.bp, rows.tp, rows.tp, 0, rows.bs * (past + rows.ts))
    o_mla = _mla_attn(qn, qr, kn, krh, vv, o_mla, heads, rows.bs, rows.ts, past + rows.ts, rows.n_p, 0)

    par = _rwkv_params(p)
    c = par['w0'].shape[1]
    o_rwkv = jnp.zeros((n, c), BF16)
    o_rwkv, sf, sb = _rwkv(z_rwkv, o_rwkv, par, rows.bp, rows.tp, 0)
    s0 = (_states_to_blockdiag(s_f0), _states_to_blockdiag(s_b0))
    o_rwkv, _, _ = _rwkv(z_rwkv, o_rwkv, par, rows.bs, rows.ts, rows.n_p, s0)

    hm = heads * MLA_V
    w_out = p['w_out'].astype(BF16)
    y = _matmul([o_mla, o_rwkv], [w_out[:hm], w_out[hm:]], res=y, m6=m6, gate_idx=2, rows=rows)
    new_ckv = ckvn[:rows.n_p].reshape(rows.bp, rows.tp, kv_rank)
    new_kr = zm[:rows.n_p, kr_off:kr_off + MLA_ROPE].reshape(rows.bp, rows.tp, MLA_ROPE)
    return y, (new_ckv, new_kr, _blockdiag_to_states(sf), _blockdiag_to_states(sb))


def _gqa_layer(y, h, m6, rows, past, p, ctx):
    n, d = y.shape
    heads = p['w_out'].shape[0] // ATTN_HEAD
    kv_heads = heads // ATTN_GROUP
    nq, nk = heads * ATTN_HEAD, kv_heads * ATTN_HEAD
    z = _matmul([h], [p['w_in'].astype(BF16)])
    cos_t, sin_t = _token_tables(rows, ATTN_HEAD)
    g2 = _pad_rows8([p['qk_norm'][0], p['qk_norm'][1]])
    q, k_keep, k_rot, v_b = _gqa_prep(z, g2, cos_t, sin_t, heads, kv_heads, float(ATTN_HEAD) ** -0.5)
    ctx_k, ctx_v = ctx
    k_all = _kv_order(rows, k_rot, ctx_k.reshape(rows.bs, past, nk).astype(BF16))
    v_all = _kv_order(rows, v_b, ctx_v.reshape(rows.bs, past, nk).astype(BF16))
    o = jnp.zeros((n, nq), BF16)
    o = _gqa_attn(q, k_all, v_all, o, kv_heads, rows.bp, rows.tp, rows.tp, 0, rows.bs * (past + rows.ts))
    o = _gqa_attn(q, k_all, v_all, o, kv_heads, rows.bs, rows.ts, past + rows.ts, rows.n_p, 0)
    y = _matmul([o], [p['w_out'].astype(BF16)], res=y, m6=m6, gate_idx=2, rows=rows)
    new_k = k_keep[:rows.n_p].reshape(rows.bp, rows.tp, kv_heads, ATTN_HEAD)
    new_v = z[:rows.n_p, nq + nk:].reshape(rows.bp, rows.tp, kv_heads, ATTN_HEAD)
    return y, (new_k, new_v)


def kernel(x_prompt, x_sample, c, c_ctx, cache_mla_ckv, cache_mla_krope, state_rwkv_fwd, state_rwkv_bwd, cache_attn_k, cache_attn_v, mod_w, mod_b, norm_mix, norm_ffn, ab_w_in, ab_w_out, mla_q_norm, mla_w_uq, mla_kv_norm, mla_w_ukv, mla_qk_norm, rwkv_mu, rwkv_w0, rwkv_w2, rwkv_a0, rwkv_a2, rwkv_g2, rwkv_k_k, rwkv_k_a, rwkv_r_k, rwkv_ln, gqa_w_in, gqa_qk_norm, gqa_w_out, moe_w_group, moe_b_group, moe_w_expert, moe_b_expert, moe_w_gate, moe_w_up, moe_w_down):
    bp, tp, d = x_prompt.shape
    bs, ts, _ = x_sample.shape
    past = cache_mla_ckv.shape[2]
    depth = mod_w.shape[0]
    rows = _Rows(bp, tp, bs, ts)
    assert bs + 1 <= 8

    cond8 = jnp.concatenate([c_ctx[None, :], c, jnp.zeros((8 - 1 - bs, d), F32)], axis=0)
    m_all = _adaln_all(cond8, mod_w, mod_b)
    y = jnp.concatenate([x_prompt.reshape(bp * tp, d), x_sample.reshape(bs * ts, d)], axis=0)

    ab_out, gqa_out = [], []
    for layer in range(depth):
        m6 = m_all[layer].reshape(8 * 6, 1, d)
        h = _norm_mod(y, norm_mix[layer], m6, rows, 0, 1)
        i = layer // 2
        if layer % 2 == 0:
            p = dict(w_in=ab_w_in[i], w_out=ab_w_out[i], q_norm=mla_q_norm[i], w_uq=mla_w_uq[i],
                     kv_norm=mla_kv_norm[i], w_ukv=mla_w_ukv[i], qk_norm=mla_qk_norm[i], mu=rwkv_mu[i],
                     w0=rwkv_w0[i], w2=rwkv_w2[i], a0=rwkv_a0[i], a2=rwkv_a2[i], g2=rwkv_g2[i],
                     k_k=rwkv_k_k[i], k_a=rwkv_k_a[i], r_k=rwkv_r_k[i], ln=rwkv_ln[i])
            y, new = _ab_layer(y, h, m6, rows, past, p, (cache_mla_ckv[:, i], cache_mla_krope[:, i],
                                                          state_rwkv_fwd[:, i], state_rwkv_bwd[:, i]))
            ab_out.append(new)
        else:
            p = dict(w_in=gqa_w_in[i], qk_norm=gqa_qk_norm[i], w_out=gqa_w_out[i])
            y, new = _gqa_layer(y, h, m6, rows, past, p, (cache_attn_k[:, i], cache_attn_v[:, i]))
            gqa_out.append(new)
        y = _moe_layer(y, norm_ffn[layer], m6, rows, moe_w_group[layer], moe_b_group[layer], moe_w_expert[layer],
                       moe_b_expert[layer], moe_w_gate[layer], moe_w_up[layer], moe_w_down[layer])

    yp = y[:rows.n_p].reshape(bp, tp, d)
    ys = y[rows.n_p:].reshape(bs, ts, d)
    stack = lambda outs, k: jnp.stack([o[k] for o in outs], axis=1)
    return (yp, ys, stack(ab_out, 0), stack(ab_out, 1), stack(ab_out, 2), stack(ab_out, 3),
            stack(gqa_out, 0), stack(gqa_out, 1))
```

```python
import functools

import numpy as np
import jax
import jax.numpy as jnp
from jax import lax
from jax.experimental import pallas as pl
from jax.experimental.pallas import tpu as pltpu

F32 = jnp.float32
BF16 = jnp.bfloat16
I32 = jnp.int32

GRID_W = 64
ROPE_THETA = 10000.0
NORM_EPS = 1e-6
GN_EPS = 64e-5
MLA_NOPE = 128
MLA_ROPE = 64
MLA_V = 128
RWKV_HEAD = 64
ATTN_HEAD = 128
ATTN_GROUP = 4
MOE_TOPK = 2
MOE_GROUP_EXPERTS = 8

LANES = 128
RWKV_CHUNK = 64
RWKV_UNROLL = 8
MOE_BLOCK_ROWS = 384
MOE_HIDDEN_BLOCK = 256
VMEM_LIMIT = 56 * 1024 * 1024
NEG_BIG = -1e30


def _pick(n, prefs):
    for p in prefs:
        if n % p == 0:
            return p
    return n


def _cparams(sem):
    return pltpu.CompilerParams(dimension_semantics=sem, vmem_limit_bytes=VMEM_LIMIT)


def _dot(a, b):
    return jnp.dot(a, b, preferred_element_type=F32)


def _dot_nt(a, b):
    return lax.dot_general(a, b, (((1,), (1,)), ((), ())), preferred_element_type=F32)


def _dot_tn(a, b):
    return lax.dot_general(a, b, (((0,), (0,)), ((), ())), preferred_element_type=F32)


def _split(x):
    hi = x.astype(BF16)
    lo = (x - hi.astype(F32)).astype(BF16)
    return hi, lo


def _dot3(a, b):
    ah, al = _split(a)
    bh, bl = _split(b)
    return _dot(ah, bh) + _dot(ah, bl) + _dot(al, bh)


def _mod_kernel(c_ref, w_ref, b_ref, o_ref):
    c = c_ref[...]
    s = c * jax.nn.sigmoid(c)
    sh, sl = _split(s)
    w = w_ref[...].astype(BF16)
    o_ref[...] = _dot(sh, w) + _dot(sl, w) + b_ref[...]


def _adaln_all(cond8, mod_w, mod_b):
    depth, d, n6 = mod_w.shape
    tn = _pick(n6, (512, 256, 128))
    return pl.pallas_call(
        _mod_kernel,
        grid=(depth, n6 // tn),
        in_specs=[pl.BlockSpec((8, d), lambda l, j: (0, 0)),
                  pl.BlockSpec((None, d, tn), lambda l, j: (l, 0, j)),
                  pl.BlockSpec((None, 1, tn), lambda l, j: (l, 0, j))],
        out_specs=pl.BlockSpec((None, 8, tn), lambda l, j: (l, 0, j)),
        out_shape=jax.ShapeDtypeStruct((depth, 8, n6), F32),
        compiler_params=_cparams(("arbitrary", "arbitrary")),
    )(cond8, mod_w, mod_b.reshape(depth, 1, n6))


class _Rows:
    def __init__(self, bp, tp, bs, ts):
        self.bp, self.tp, self.bs, self.ts = bp, tp, bs, ts
        self.n_p = bp * tp
        self.n = bp * tp + bs * ts

    def group_of_block(self, tm):
        n_p, ts = self.n_p, self.ts
        assert n_p % tm == 0 and ts % tm == 0
        return lambda i: jnp.where(i * tm < n_p, 0, 1 + (i * tm - n_p) // ts)


def _norm_mod_kernel(y_ref, g_ref, sh_ref, sc_ref, o_ref):
    y = y_ref[...]
    ms = jnp.mean(y * y, axis=-1, keepdims=True)
    h = y * lax.rsqrt(ms + NORM_EPS) * g_ref[...]
    h = h * (1.0 + sc_ref[...]) + sh_ref[...]
    o_ref[...] = h.astype(o_ref.dtype)


def _norm_mod(y, gain, m6, rows, shift_idx, scale_idx):
    n, d = y.shape
    tm = _pick(rows.tp, (256, 128, 64, 32, 16, 8))
    tm = tm if rows.ts % tm == 0 else _pick(np.gcd(rows.tp, rows.ts), (256, 128, 64, 32, 16, 8))
    grp = rows.group_of_block(tm)
    return pl.pallas_call(
        _norm_mod_kernel,
        grid=(n // tm,),
        in_specs=[pl.BlockSpec((tm, d), lambda i: (i, 0)),
                  pl.BlockSpec((1, d), lambda i: (0, 0)),
                  pl.BlockSpec((None, 1, d), lambda i: (grp(i) * 6 + shift_idx, 0, 0)),
                  pl.BlockSpec((None, 1, d), lambda i: (grp(i) * 6 + scale_idx, 0, 0))],
        out_specs=pl.BlockSpec((tm, d), lambda i: (i, 0)),
        out_shape=jax.ShapeDtypeStruct((n, d), BF16),
        compiler_params=_cparams(("arbitrary",)),
    )(y, gain.reshape(1, d), m6, m6)


def _mm_kernel(*refs, n_in, has_res):
    o_ref = refs[-1]
    acc = None
    for i in range(n_in):
        w = refs[n_in + i][...]
        if w.dtype != BF16:
            w = w.astype(BF16)
        part = _dot(refs[i][...], w)
        acc = part if acc is None else acc + part
    if has_res:
        acc = refs[2 * n_in][...] + refs[2 * n_in + 1][...] * acc
    o_ref[...] = acc.astype(o_ref.dtype)


def _matmul(xs, ws, out_dtype=F32, res=None, m6=None, gate_idx=None, rows=None, tm_prefs=(1024, 512, 256, 128, 64, 32, 16, 8)):
    m = xs[0].shape[0]
    n = ws[0].shape[1]
    if res is not None:
        cands = [t for t in tm_prefs if rows.n_p % t == 0 and rows.ts % t == 0]
        tm = cands[0]
    else:
        tm = _pick(m, tm_prefs)
    tn = _pick(n, (512, 384, 256, 128))
    in_specs = [pl.BlockSpec((tm, x.shape[1]), lambda i, j: (i, 0)) for x in xs]
    in_specs += [pl.BlockSpec((w.shape[0], tn), lambda i, j: (0, j)) for w in ws]
    args = list(xs) + list(ws)
    if res is not None:
        grp = rows.group_of_block(tm)
        in_specs += [pl.BlockSpec((tm, tn), lambda i, j: (i, j)),
                     pl.BlockSpec((None, 1, tn), lambda i, j: (grp(i) * 6 + gate_idx, 0, j))]
        args += [res, m6]
    return pl.pallas_call(
        functools.partial(_mm_kernel, n_in=len(xs), has_res=res is not None),
        grid=(m // tm, n // tn),
        in_specs=in_specs,
        out_specs=pl.BlockSpec((tm, tn), lambda i, j: (i, j)),
        out_shape=jax.ShapeDtypeStruct((m, n), out_dtype),
        compiler_params=_cparams(("arbitrary", "arbitrary")),
    )(*args)


def _mla_mid_kernel(z_ref, gq_ref, gkv_ref, cq_ref, ckv_ref, ckvb_ref, *, q_rank, kv_rank):
    cq = z_ref[:, :q_rank]
    ms = jnp.mean(cq * cq, axis=-1, keepdims=True)
    cq_ref[...] = (cq * lax.rsqrt(ms + NORM_EPS) * gq_ref[...]).astype(BF16)
    ckv = z_ref[:, q_rank:q_rank + kv_rank]
    ms = jnp.mean(ckv * ckv, axis=-1, keepdims=True)
    ckvn = ckv * lax.rsqrt(ms + NORM_EPS) * gkv_ref[...]
    ckv_ref[...] = ckvn
    ckvb_ref[...] = ckvn.astype(BF16)


def _mla_mid(zm, q_norm, kv_norm, q_rank, kv_rank):
    n, w = zm.shape
    tm = _pick(n, (512, 256, 128, 64, 32, 16, 8))
    return pl.pallas_call(
        functools.partial(_mla_mid_kernel, q_rank=q_rank, kv_rank=kv_rank),
        grid=(n // tm,),
        in_specs=[pl.BlockSpec((tm, w), lambda i: (i, 0)),
                  pl.BlockSpec((1, q_rank), lambda i: (0, 0)),
                  pl.BlockSpec((1, kv_rank), lambda i: (0, 0))],
        out_specs=[pl.BlockSpec((tm, q_rank), lambda i: (i, 0)),
                   pl.BlockSpec((tm, kv_rank), lambda i: (i, 0)),
                   pl.BlockSpec((tm, kv_rank), lambda i: (i, 0))],
        out_shape=[jax.ShapeDtypeStruct((n, q_rank), BF16),
                   jax.ShapeDtypeStruct((n, kv_rank), F32),
                   jax.ShapeDtypeStruct((n, kv_rank), BF16)],
        compiler_params=_cparams(("arbitrary",)),
    )(zm, q_norm.reshape(1, q_rank), kv_norm.reshape(1, kv_rank))


def _low_half_mask(shape):
    return lax.broadcasted_iota(I32, shape, 1) < (LANES // 2)


def _mla_q_prep_kernel(z_ref, g_ref, cos_ref, sin_ref, qn_ref, qr_ref, *, heads, scale):
    hn = heads * MLA_NOPE
    hr = heads * MLA_ROPE
    qk = float(MLA_NOPE + MLA_ROPE)
    lo = _low_half_mask(cos_ref.shape)
    cos = cos_ref[...]
    sin = sin_ref[...]
    g_n, g_r, g_s = g_ref[0:1, :], g_ref[1:2, :], g_ref[2:3, :]
    for p in range(heads // 2):
        rope = z_ref[:, hn + LANES * p:hn + LANES * (p + 1)]
        rsw = z_ref[:, hn + hr + LANES * p:hn + hr + LANES * (p + 1)]
        r2 = rope * rope
        ss_r = (jnp.sum(jnp.where(lo, r2, 0.0), axis=-1, keepdims=True),
                jnp.sum(jnp.where(lo, 0.0, r2), axis=-1, keepdims=True))
        inv = []
        for hh in range(2):
            h = 2 * p + hh
            nope = z_ref[:, MLA_NOPE * h:MLA_NOPE * (h + 1)]
            ss = jnp.sum(nope * nope, axis=-1, keepdims=True) + ss_r[hh]
            inv_h = lax.rsqrt(ss / qk + NORM_EPS) * scale
            inv.append(inv_h)
            qn_ref[:, MLA_NOPE * h:MLA_NOPE * (h + 1)] = (nope * inv_h * g_n).astype(BF16)
        inv2 = jnp.where(lo, inv[0], inv[1])
        qr_ref[:, LANES * p:LANES * (p + 1)] = ((rope * g_r * cos + rsw * g_s * sin) * inv2).astype(BF16)


def _mla_q_prep(zq, g3, cos, sin, heads, scale):
    n, w = zq.shape
    tm = _pick(n, (256, 128, 64, 32, 16, 8))
    return pl.pallas_call(
        functools.partial(_mla_q_prep_kernel, heads=heads, scale=scale),
        grid=(n // tm,),
        in_specs=[pl.BlockSpec((tm, w), lambda i: (i, 0)),
                  pl.BlockSpec((8, LANES), lambda i: (0, 0)),
                  pl.BlockSpec((tm, LANES), lambda i: (i, 0)),
                  pl.BlockSpec((tm, LANES), lambda i: (i, 0))],
        out_specs=[pl.BlockSpec((tm, heads * MLA_NOPE), lambda i: (i, 0)),
                   pl.BlockSpec((tm, heads * MLA_ROPE), lambda i: (i, 0))],
        out_shape=[jax.ShapeDtypeStruct((n, heads * MLA_NOPE), BF16),
                   jax.ShapeDtypeStruct((n, heads * MLA_ROPE), BF16)],
        compiler_params=_cparams(("arbitrary",)),
    )(zq, g3, cos, sin)


def _mla_k_prep_kernel(z_ref, kr_ref, krs_ref, g_ref, cos_ref, sin_ref, kn_ref, krh_ref, v_ref, *, heads):
    hn = heads * MLA_NOPE
    qk = float(MLA_NOPE + MLA_ROPE)
    lo = _low_half_mask(cos_ref.shape)
    g_n, g_r, g_s = g_ref[0:1, :], g_ref[1:2, :], g_ref[2:3, :]
    kr = kr_ref[...]
    ss_r = jnp.sum(jnp.where(lo, kr * kr, 0.0), axis=-1, keepdims=True)
    base = kr * g_r * cos_ref[...] + krs_ref[...] * g_s * sin_ref[...]
    for p in range(heads // 2):
        inv = []
        for hh in range(2):
            h = 2 * p + hh
            nope = z_ref[:, MLA_NOPE * h:MLA_NOPE * (h + 1)]
            ss = jnp.sum(nope * nope, axis=-1, keepdims=True) + ss_r
            inv_h = lax.rsqrt(ss / qk + NORM_EPS)
            inv.append(inv_h)
            kn_ref[:, MLA_NOPE * h:MLA_NOPE * (h + 1)] = (nope * inv_h * g_n).astype(BF16)
        krh_ref[:, LANES * p:LANES * (p + 1)] = (base * jnp.where(lo, inv[0], inv[1])).astype(BF16)
    v_ref[...] = z_ref[:, hn:].astype(BF16)


def _mla_k_prep(zkv, kr2, krs2, g3, cos, sin, heads):
    n, w = zkv.shape
    tm = _pick(n, (256, 128, 64, 32, 16, 8))
    hn = heads * MLA_NOPE
    row = lambda i: (i, 0)
    return pl.pallas_call(
        functools.partial(_mla_k_prep_kernel, heads=heads),
        grid=(n // tm,),
        in_specs=[pl.BlockSpec((tm, w), row), pl.BlockSpec((tm, LANES), row), pl.BlockSpec((tm, LANES), row),
                  pl.BlockSpec((8, LANES), lambda i: (0, 0)),
                  pl.BlockSpec((tm, LANES), row), pl.BlockSpec((tm, LANES), row)],
        out_specs=[pl.BlockSpec((tm, hn), row), pl.BlockSpec((tm, heads * MLA_ROPE), row),
                   pl.BlockSpec((tm, hn), row)],
        out_shape=[jax.ShapeDtypeStruct((n, hn), BF16), jax.ShapeDtypeStruct((n, heads * MLA_ROPE), BF16),
                   jax.ShapeDtypeStruct((n, hn), BF16)],
        compiler_params=_cparams(("arbitrary",)),
    )(zkv, kr2, krs2, g3, cos, sin)


def _softmax_pv(s, v):
    m = jnp.max(s, axis=-1, keepdims=True)
    p = jnp.exp(s - m)
    l = jnp.sum(p, axis=-1, keepdims=True)
    return _dot(p.astype(BF16), v) / l


def _mla_attn_kernel(qn_ref, qr_ref, kn_ref, kr_ref, v_ref, _aliased_out, o_ref):
    lo = _low_half_mask(qr_ref.shape)
    qr = qr_ref[...]
    kr = kr_ref[...]
    zero = jnp.zeros_like(qr)
    for hh in range(2):
        sl = slice(MLA_NOPE * hh, MLA_NOPE * (hh + 1))
        qr_h = jnp.where(lo, qr, zero) if hh == 0 else jnp.where(lo, zero, qr)
        s = _dot_nt(qn_ref[:, sl], kn_ref[:, sl]) + _dot_nt(qr_h, kr)
        o_ref[:, sl] = _softmax_pv(s, v_ref[:, sl]).astype(o_ref.dtype)


def _mla_attn(qn, qr, kn, krh, v, out, heads, nb, tq_len, tk_len, q_row0, kv_row0):
    tq = _pick(tq_len, (256, 128, 64, 32, 16, 8))
    nq = tq_len // tq
    assert q_row0 % tq == 0 and kv_row0 % tk_len == 0
    qb0, kb0 = q_row0 // tq, kv_row0 // tk_len
    qmap = lambda b, p, i: (qb0 + b * nq + i, p)
    kmap = lambda b, p, i: (kb0 + b, p)
    return pl.pallas_call(
        _mla_attn_kernel,
        grid=(nb, heads // 2, nq),
        in_specs=[pl.BlockSpec((tq, 2 * MLA_NOPE), qmap), pl.BlockSpec((tq, LANES), qmap),
                  pl.BlockSpec((tk_len, 2 * MLA_NOPE), kmap), pl.BlockSpec((tk_len, LANES), kmap),
                  pl.BlockSpec((tk_len, 2 * MLA_V), kmap),
                  pl.BlockSpec(memory_space=pl.ANY)],
        out_specs=pl.BlockSpec((tq, 2 * MLA_V), qmap),
        out_shape=jax.ShapeDtypeStruct(out.shape, out.dtype),
        input_output_aliases={5: 0},
        compiler_params=_cparams(("arbitrary", "arbitrary", "arbitrary")),
    )(qn, qr, kn, krh, v, out)


def _gqa_prep_kernel(z_ref, g_ref, cos_ref, sin_ref, q_ref, kk_ref, kr_ref, v_ref, *, heads, kv_heads, scale):
    cos = cos_ref[...]
    sin = sin_ref[...]
    g_q, g_k = g_ref[0:1, :], g_ref[1:2, :]

    def normed(x, g):
        ms = jnp.mean(x * x, axis=-1, keepdims=True)
        return x * lax.rsqrt(ms + NORM_EPS) * g

    def rope(x):
        return x * cos + pltpu.roll(x, ATTN_HEAD // 2, 1) * sin

    for h in range(heads):
        sl = slice(ATTN_HEAD * h, ATTN_HEAD * (h + 1))
        q_ref[:, sl] = (rope(normed(z_ref[:, sl], g_q)) * scale).astype(BF16)
    for h in range(kv_heads):
        sl = slice(ATTN_HEAD * h, ATTN_HEAD * (h + 1))
        kn = normed(z_ref[:, ATTN_HEAD * (heads + h):ATTN_HEAD * (heads + h + 1)], g_k)
        kk_ref[:, sl] = kn
        kr_ref[:, sl] = rope(kn).astype(BF16)
    v_ref[...] = z_ref[:, ATTN_HEAD * (heads + kv_heads):].astype(BF16)


def _gqa_prep(z, g2, cos, sin, heads, kv_heads, scale):
    n, w = z.shape
    tm = _pick(n, (256, 128, 64, 32, 16, 8))
    row = lambda i: (i, 0)
    nq, nk = heads * ATTN_HEAD, kv_heads * ATTN_HEAD
    return pl.pallas_call(
        functools.partial(_gqa_prep_kernel, heads=heads, kv_heads=kv_heads, scale=scale),
        grid=(n // tm,),
        in_specs=[pl.BlockSpec((tm, w), row), pl.BlockSpec((8, LANES), lambda i: (0, 0)),
                  pl.BlockSpec((tm, LANES), row), pl.BlockSpec((tm, LANES), row)],
        out_specs=[pl.BlockSpec((tm, nq), row), pl.BlockSpec((tm, nk), row),
                   pl.BlockSpec((tm, nk), row), pl.BlockSpec((tm, nk), row)],
        out_shape=[jax.ShapeDtypeStruct((n, nq), BF16), jax.ShapeDtypeStruct((n, nk), F32),
                   jax.ShapeDtypeStruct((n, nk), BF16), jax.ShapeDtypeStruct((n, nk), BF16)],
        compiler_params=_cparams(("arbitrary",)),
    )(z, g2, cos, sin)


def _gqa_attn_kernel(q_ref, k_ref, v_ref, _aliased_out, o_ref):
    k = k_ref[...]
    v = v_ref[...]
    for g in range(ATTN_GROUP):
        sl = slice(ATTN_HEAD * g, ATTN_HEAD * (g + 1))
        o_ref[:, sl] = _softmax_pv(_dot_nt(q_ref[:, sl], k), v).astype(o_ref.dtype)


def _gqa_attn(q, k, v, out, kv_heads, nb, tq_len, tk_len, q_row0, kv_row0):
    tq = _pick(tq_len, (256, 128, 64, 32, 16, 8))
    nq = tq_len // tq
    assert q_row0 % tq == 0 and kv_row0 % tk_len == 0
    qb0, kb0 = q_row0 // tq, kv_row0 // tk_len
    qmap = lambda b, h, i: (qb0 + b * nq + i, h)
    kmap = lambda b, h, i: (kb0 + b, h)
    gw = ATTN_GROUP * ATTN_HEAD
    return pl.pallas_call(
        _gqa_attn_kernel,
        grid=(nb, kv_heads, nq),
        in_specs=[pl.BlockSpec((tq, gw), qmap), pl.BlockSpec((tk_len, ATTN_HEAD), kmap),
                  pl.BlockSpec((tk_len, ATTN_HEAD), kmap), pl.BlockSpec(memory_space=pl.ANY)],
        out_specs=pl.BlockSpec((tq, gw), qmap),
        out_shape=jax.ShapeDtypeStruct(out.shape, out.dtype),
        input_output_aliases={3: 0},
        compiler_params=_cparams(("arbitrary", "arbitrary", "arbitrary")),
    )(q, k, v, out)


def _seg_sum(x, lo):
    s0 = jnp.sum(jnp.where(lo, x, 0.0), axis=-1, keepdims=True)
    s1 = jnp.sum(jnp.where(lo, 0.0, x), axis=-1, keepdims=True)
    return jnp.where(lo, s0, s1)


def _centred_shift(z, mu_ref):
    t = z.shape[0]
    ti = lax.broadcasted_iota(I32, z.shape, 0)
    prev = jnp.where(ti == 0, 0.0, pltpu.roll(z, 1, 0))
    nxt = jnp.where(ti == t - 1, 0.0, pltpu.roll(z, t - 1, 0))
    return z + mu_ref[0:1, :] * (prev - z) + mu_ref[1:2, :] * (nxt - z)


def _stack_heads(x, lo):
    return jnp.concatenate([jnp.where(lo, x, 0.0), jnp.where(lo, 0.0, x)], axis=0)


def _rwkv_chunks(units, r_s, v_s, kk_s, lw_s, kh_s, bb_s, consts):
    L = RWKV_CHUNK
    U = range(len(units))
    cst = [consts[d] for d, _ in units]
    rows = [pl.ds(c0, L) for _, c0 in units]
    lo, eye = cst[0][4], cst[0][3]
    lw = [lw_s[units[u][0], rows[u], :] for u in U]
    hi = [x.astype(BF16) for x in lw]
    r1 = [lw[u] - hi[u].astype(F32) for u in U]
    mid = [x.astype(BF16) for x in r1]
    low = [(r1[u] - mid[u].astype(F32)).astype(BF16) for u in U]
    cum = [_dot(cst[u][0], hi[u]) + _dot(cst[u][0], mid[u]) + _dot(cst[u][0], low[u]) for u in U]
    g_inc = [jnp.exp(x) for x in cum]
    g_inv = [jnp.exp(-x) for x in cum]
    kap = [_stack_heads(kk_s[rows[u], :] * jnp.exp(cum[u] - lw[u]), lo).astype(BF16) for u in U]
    g_end = [g_inc[u][L - 1:L, :] if units[u][0] == 0 else g_inc[u][0:1, :] for u in U]
    bt32 = [_stack_heads(bb_s[units[u][0], rows[u], :] * g_inv[u], lo) for u in U]
    kt32 = [_stack_heads(kh_s[units[u][0], rows[u], :] * g_inv[u], lo) for u in U]
    rt32 = [_stack_heads(r_s[rows[u], :] * g_inc[u], lo) for u in U]
    bt = [x.astype(BF16) for x in bt32]
    kt = [x.astype(BF16) for x in kt32]
    rt = [x.astype(BF16) for x in rt32]
    vr = [_stack_heads(v_s[rows[u], :], lo).astype(BF16) for u in U]
    lb = [jnp.where(cst[u][1], _dot_nt(kap[u], bt[u]), 0.0) for u in U]
    lk = [jnp.where(cst[u][1], _dot_nt(kap[u], kt[u]), 0.0).astype(BF16) for u in U]
    hb = [jnp.where(cst[u][2], _dot_nt(rt[u], bt[u]), 0.0).astype(BF16) for u in U]
    hk = [jnp.where(cst[u][2], _dot_nt(rt[u], kt[u]), 0.0).astype(BF16) for u in U]
    x = [jnp.where(eye, 1.0, 0.0) - lb[u] for u in U]
    lbb = [y.astype(BF16) for y in lb]
    p = [_dot(y, y) for y in lbb]
    lkv = [_dot(lk[u], vr[u]).astype(BF16) for u in U]
    n_iter = int(np.log2(L)) - 1
    for it in range(n_iter):
        pb = [y.astype(BF16) for y in p]
        x = [x[u] + _dot(x[u].astype(BF16), pb[u]) for u in U]
        if it + 1 < n_iter:
            p = [_dot(y, y) for y in pb]
    xb = [y.astype(BF16) for y in x]
    a1 = [(-_dot(xb[u], kap[u])).astype(BF16) for u in U]
    a2 = [(-_dot(xb[u], lkv[u])).astype(BF16) for u in U]
    btg = [(bt32[u] * g_end[u]).astype(BF16) for u in U]
    ktg = [(kt32[u] * g_end[u]).astype(BF16) for u in U]
    m = [jnp.where(eye, jnp.broadcast_to(g_end[u], eye.shape), 0.0) + _dot_tn(btg[u], a1[u]) for u in U]
    nn = [_dot_tn(btg[u], a2[u]) + _dot_tn(ktg[u], vr[u]) for u in U]
    y1 = [rt32[u] + _dot(hb[u], a1[u]) for u in U]
    y2 = [_dot(hb[u], a2[u]) + _dot(hk[u], vr[u]) for u in U]
    return [(m[u].astype(BF16), nn[u], y1[u].astype(BF16), y2[u]) for u in U]


def _rwkv_apply(st, local):
    m, nn, y1, y2 = local
    L = RWKV_CHUNK
    stb = st.astype(BF16)
    ysm = _dot(y1, stb) + y2
    return ysm[:L, :] + ysm[L:, :], _dot(m, stb) + nn


def _rwkv_kernel(*refs, t_len, has_s0):
    (zr_ref, zk_ref, zv_ref, zs_ref, mur_ref, muk_ref, muv_ref, mus_ref, w0_ref, w2_ref, a0_ref, a2_ref,
     g2_ref, kkg_ref, ka_ref, rk_ref, ln_ref) = refs[:17]
    pos = 17
    if has_s0:
        s0f_ref, s0b_ref = refs[17:19]
        pos = 19
    o_ref, sf_ref, sb_ref = refs[pos:pos + 3]
    r_s, v_s, kk_s, bonus_s, g_s, lw_s, kh_s, bb_s, y_s = refs[pos + 3:]
    L = RWKV_CHUNK
    n_chunks = t_len // L
    lo = _low_half_mask((t_len, LANES))

    r = _centred_shift(zr_ref[...], mur_ref)
    k = _centred_shift(zk_ref[...], muk_ref)
    v = _centred_shift(zv_ref[...], muv_ref)
    s = _centred_shift(zs_ref[...], mus_ref)
    tw = jnp.tanh(s[:, 0:LANES])
    ad = s[:, LANES:2 * LANES]
    g_s[...] = _dot3(jax.nn.sigmoid(s[:, 2 * LANES:]), g2_ref[...])
    kk = k * kkg_ref[...]
    kk = kk / jnp.maximum(jnp.sqrt(_seg_sum(kk * kk, lo)), 1e-12)
    kh_sum = jnp.zeros_like(k)
    for d in range(2):
        keep = lo if d == 0 else jnp.logical_not(lo)
        w_pre = w0_ref[d:d + 1, :] + _dot3(jnp.where(keep, tw, 0.0), w2_ref[...])
        neg = -w_pre
        softplus = jnp.maximum(neg, 0.0) + jnp.log1p(jnp.exp(-jnp.abs(neg)))
        lw_s[d] = -jnp.exp(-softplus - 0.5)
        a = jax.nn.sigmoid(a0_ref[d:d + 1, :] + _dot3(jnp.where(keep, ad, 0.0), a2_ref[...]))
        kh = k * (1.0 + (a - 1.0) * ka_ref[...])
        kh_s[d] = kh
        bb_s[d] = kk * a
        kh_sum = kh_sum + kh
    bonus_s[...] = _seg_sum(r * kh_sum * rk_ref[...], lo) * v
    r_s[...] = r
    v_s[...] = v
    kk_s[...] = kk

    ri = lax.broadcasted_iota(I32, (2 * L, 2 * L), 0)
    ci = lax.broadcasted_iota(I32, (2 * L, 2 * L), 1)
    tt, ss = ri & (L - 1), ci & (L - 1)
    eye = ri == ci
    ti = lax.broadcasted_iota(I32, (L, L), 0)
    si = lax.broadcasted_iota(I32, (L, L), 1)
    lo_c = _low_half_mask((L, LANES))
    consts = ((jnp.where(si <= ti, 1.0, 0.0).astype(BF16), tt > ss, tt >= ss, eye, lo_c),
              (jnp.where(si >= ti, 1.0, 0.0).astype(BF16), tt < ss, tt <= ss, eye, lo_c))

    if has_s0:
        sf_ref[...] = s0f_ref[...]
        sb_ref[...] = s0b_ref[...]
    else:
        sf_ref[...] = jnp.zeros((2 * L, 2 * L), F32)
        sb_ref[...] = jnp.zeros((2 * L, 2 * L), F32)

    unroll = max(u for u in range(1, RWKV_UNROLL + 1) if n_chunks % u == 0)

    def group(i, carry):
        units = []
        for u in range(unroll):
            c = i * unroll + u
            starts = (c * L, (n_chunks - 1 - c) * L)
            if not isinstance(i, int):
                starts = tuple(pl.multiple_of(s, L) for s in starts)
            units += [(0, starts[0]), (1, starts[1])]
        local = _rwkv_chunks(units, r_s, v_s, kk_s, lw_s, kh_s, bb_s, consts)
        states = [sf_ref[...], sb_ref[...]]
        for (d, start), loc in zip(units, local):
            y_d, states[d] = _rwkv_apply(states[d], loc)
            y_s[d, pl.ds(start, L), :] = y_d
        sf_ref[...] = states[0]
        sb_ref[...] = states[1]
        return carry

    if n_chunks == unroll:
        group(0, 0)
    else:
        lax.fori_loop(0, n_chunks // unroll, group, 0)

    y = y_s[0] + y_s[1]
    inv_n = 1.0 / RWKV_HEAD
    mu = _seg_sum(y, lo) * inv_n
    yc = y - mu
    var = _seg_sum(yc * yc, lo) * inv_n
    yn = yc * lax.rsqrt(var + GN_EPS) * ln_ref[0:1, :] + ln_ref[1:2, :]
    o_ref[...] = ((yn + bonus_s[...]) * g_s[...]).astype(o_ref.dtype)


def _rwkv(z, out, par, nb, t_len, row0, s0=None):
    c = par['w0'].shape[1]
    pairs = c // LANES
    assert t_len % RWKV_CHUNK == 0 and row0 % t_len == 0 and (3 * c) % (4 * LANES) == 0
    rb0 = row0 // t_len
    cb = c // LANES
    sb = 3 * c // (4 * LANES)
    zmap = lambda off: (lambda b, j: (rb0 + b, off + j))
    cmap = lambda off: (lambda b, j: (0, off + j))
    in_specs = [pl.BlockSpec((t_len, LANES), zmap(0)), pl.BlockSpec((t_len, LANES), zmap(cb)),
                pl.BlockSpec((t_len, LANES), zmap(2 * cb)),
                pl.BlockSpec((t_len, 4 * LANES), lambda b, j: (rb0 + b, sb)),
                pl.BlockSpec((2, LANES), cmap(0)), pl.BlockSpec((2, LANES), cmap(cb)),
                pl.BlockSpec((2, LANES), cmap(2 * cb)), pl.BlockSpec((2, 4 * LANES), lambda b, j: (0, sb)),
                pl.BlockSpec((2, LANES), cmap(0)), pl.BlockSpec((LANES, LANES), cmap(0)),
                pl.BlockSpec((2, LANES), cmap(0)), pl.BlockSpec((LANES, LANES), cmap(0)),
                pl.BlockSpec((2 * LANES, LANES), cmap(0)),
                pl.BlockSpec((1, LANES), cmap(0)), pl.BlockSpec((1, LANES), cmap(0)),
                pl.BlockSpec((1, LANES), cmap(0)), pl.BlockSpec((2, LANES), cmap(0))]
    args = [z, z, z, z, par['mu'], par['mu'], par['mu'], par['mu'], par['w0'], par['w2'], par['a0'], par['a2'],
            par['g2'], par['k_k'], par['k_a'], par['r_k'], par['ln']]
    smap = lambda b, j: (b, j, 0, 0)
    if s0 is not None:
        in_specs += [pl.BlockSpec((None, None, LANES, LANES), smap)] * 2
        args += [s0[0], s0[1]]
    in_specs.append(pl.BlockSpec(memory_space=pl.ANY))
    args.append(out)
    st_shape = jax.ShapeDtypeStruct((nb, pairs, LANES, LANES), F32)
    tl = pltpu.VMEM((t_len, LANES), F32)
    tl2 = pltpu.VMEM((2, t_len, LANES), F32)

    def kern(*refs):
        n_in = len(args)
        _rwkv_kernel(*refs[:n_in - 1], *refs[n_in:], t_len=t_len, has_s0=s0 is not None)

    return pl.pallas_call(
        kern,
        grid=(nb, pairs),
        in_specs=in_specs,
        out_specs=[pl.BlockSpec((t_len, LANES), zmap(0)),
                   pl.BlockSpec((None, None, LANES, LANES), smap),
                   pl.BlockSpec((None, None, LANES, LANES), smap)],
        out_shape=[jax.ShapeDtypeStruct(out.shape, out.dtype), st_shape, st_shape],
        scratch_shapes=[tl, tl, tl, tl, tl, tl2, tl2, tl2, tl2],
        input_output_aliases={len(args) - 1: 0},
        compiler_params=_cparams(("arbitrary", "arbitrary")),
    )(*args)


def _rwkv_params(p):
    c = p['w0'].shape[1]
    return dict(mu=p['mu'], w0=p['w0'], w2=p['w2'].reshape(-1, c), a0=p['a0'], a2=p['a2'].reshape(-1, c),
                g2=p['g2'], k_k=p['k_k'].reshape(1, c), k_a=p['k_a'].reshape(1, c), r_k=p['r_k'].reshape(1, c),
                ln=p['ln'])


def _states_to_blockdiag(s):
    b, h, n, _ = s.shape
    st = jnp.swapaxes(s, 2, 3).reshape(b, h // 2, 2, n, n)
    z = jnp.zeros((b, h // 2, n, n), s.dtype)
    top = jnp.concatenate([st[:, :, 0], z], axis=-1)
    bot = jnp.concatenate([z, st[:, :, 1]], axis=-1)
    return jnp.concatenate([top, bot], axis=-2)


def _blockdiag_to_states(bd):
    b, hp, _, _ = bd.shape
    n = RWKV_HEAD
    st = jnp.stack([bd[:, :, :n, :n], bd[:, :, n:, n:]], axis=2)
    return jnp.swapaxes(st, 3, 4).reshape(b, 2 * hp, n, n)


def _ffn_pre_kernel(y_ref, g_ref, sh_ref, sc_ref, wh_ref, wl_ref, b_ref, hp_ref, route_ref, *, n_exp, n_grp):
    y = y_ref[...]
    ms = jnp.mean(y * y, axis=-1, keepdims=True)
    h = y * lax.rsqrt(ms + NORM_EPS) * g_ref[...]
    h = h * (1.0 + sc_ref[...]) + sh_ref[...]
    hp_ref[...] = h
    hh, hl = _split(h)
    wh = wh_ref[...]
    lg = _dot(hh, wh) + _dot(hl, wh) + _dot(hh, wl_ref[...]) + b_ref[...]
    lane = lax.broadcasted_iota(I32, lg.shape, 1)
    lane_f = lane.astype(F32)
    big = 1e9
    is_g = (lane >= n_exp) & (lane < n_exp + n_grp)
    gl = jnp.where(is_g, lg, NEG_BIG)
    gmax = jnp.max(gl, axis=-1, keepdims=True)
    gidx = jnp.min(jnp.where(is_g & (gl == gmax), lane_f, big), axis=-1, keepdims=True) - n_exp
    gprob = 1.0 / jnp.sum(jnp.where(is_g, jnp.exp(gl - gmax), 0.0), axis=-1, keepdims=True)
    gstart = gidx * MOE_GROUP_EXPERTS
    in_grp = (lane_f >= gstart) & (lane_f < gstart + MOE_GROUP_EXPERTS)
    el = jnp.where(in_grp, lg, NEG_BIG)
    e1v = jnp.max(el, axis=-1, keepdims=True)
    e1i = jnp.min(jnp.where(in_grp & (el == e1v), lane_f, big), axis=-1, keepdims=True)
    rest = in_grp & (lane_f != e1i)
    el2 = jnp.where(rest, lg, NEG_BIG)
    e2v = jnp.max(el2, axis=-1, keepdims=True)
    e2i = jnp.min(jnp.where(rest & (el2 == e2v), lane_f, big), axis=-1, keepdims=True)
    t = jnp.exp(e2v - e1v)
    den = 1.0 + t
    w1 = gprob / den
    w2 = gprob * t / den
    route_ref[...] = jnp.where(lane == 0, e1i, jnp.where(lane == 1, e2i,
                               jnp.where(lane == 2, w1, jnp.where(lane == 3, w2, 0.0))))


def _ffn_pre(y, gain, m6, rows, wr_hi, wr_lo, br, n_exp, n_grp):
    n, d = y.shape
    tm = _pick(np.gcd(rows.n_p, rows.ts), (256, 128, 64, 32, 16, 8))
    grp = rows.group_of_block(tm)
    const = lambda i: (0, 0)
    return pl.pallas_call(
        functools.partial(_ffn_pre_kernel, n_exp=n_exp, n_grp=n_grp),
        grid=(n // tm,),
        in_specs=[pl.BlockSpec((tm, d), lambda i: (i, 0)),
                  pl.BlockSpec((1, d), const),
                  pl.BlockSpec((None, 1, d), lambda i: (grp(i) * 6 + 3, 0, 0)),
                  pl.BlockSpec((None, 1, d), lambda i: (grp(i) * 6 + 4, 0, 0)),
                  pl.BlockSpec((d, LANES), const), pl.BlockSpec((d, LANES), const),
                  pl.BlockSpec((1, LANES), const)],
        out_specs=[pl.BlockSpec((tm, d), lambda i: (i, 0)), pl.BlockSpec((tm, LANES), lambda i: (i, 0))],
        out_shape=[jax.ShapeDtypeStruct((n, d), F32), jax.ShapeDtypeStruct((n, LANES), F32)],
        compiler_params=_cparams(("arbitrary",)),
    )(y, gain.reshape(1, d), m6, m6, wr_hi, wr_lo, br)


def _dispatch_kernel(dest_ref, hp_ref, _zeros_ref, xs_ref, sem, *, tm):
    base = pl.program_id(0) * tm

    def row_copy(src_row, dst_row):
        return pltpu.make_async_copy(hp_ref.at[pl.ds(src_row, 1), :], xs_ref.at[pl.ds(dst_row, 1), :], sem)

    def start(r, carry):
        a = MOE_TOPK * (base + r)
        for k in range(MOE_TOPK):
            row_copy(r, dest_ref[a + k]).start()
        return carry

    def wait(r, carry):
        for k in range(MOE_TOPK):
            row_copy(0, 0).wait()
        return carry

    lax.fori_loop(0, tm, start, 0)
    lax.fori_loop(0, tm, wait, 0)


def _moe_dispatch(dest, hp, n_slots):
    n = hp.shape[0]
    tm = _pick(n, (256, 128, 64, 32, 16, 8))
    any_spec = pl.BlockSpec(memory_space=pl.ANY)
    return pl.pallas_call(
        functools.partial(_dispatch_kernel, tm=tm),
        grid_spec=pltpu.PrefetchScalarGridSpec(
            num_scalar_prefetch=1, grid=(n // tm,),
            in_specs=[pl.BlockSpec((tm, hp.shape[1]), lambda i, dest: (i, 0)), any_spec], out_specs=any_spec,
            scratch_shapes=[pltpu.SemaphoreType.DMA]),
        out_shape=jax.ShapeDtypeStruct((n_slots, hp.shape[1]), hp.dtype),
        input_output_aliases={2: 0},
        compiler_params=_cparams(("arbitrary",)),
    )(dest, hp, jnp.zeros((n_slots, hp.shape[1]), hp.dtype))


def _expert_kernel(blk_e_ref, nact_ref, x_ref, wg_ref, wu_ref, wd_ref, o_ref, xb_ref):
    i = pl.program_id(0)
    j = pl.program_id(1)

    @pl.when(i < nact_ref[0])
    def _():
        @pl.when(j == 0)
        def _():
            xb_ref[...] = x_ref[...].astype(BF16)

        xb = xb_ref[...]
        gate = _dot(xb, wg_ref[...].astype(BF16))
        hid = (gate * jax.nn.sigmoid(gate) * _dot(xb, wu_ref[...].astype(BF16))).astype(BF16)

        d = o_ref.shape[1]
        cw = _pick(d, (1024, 512, 256, 128))
        for c in range(d // cw):
            cols = slice(c * cw, (c + 1) * cw)
            part = _dot(hid, wd_ref[:, cols].astype(BF16))

            @pl.when(j == 0)
            def _():
                o_ref[:, cols] = part

            @pl.when(j > 0)
            def _():
                o_ref[:, cols] += part


def _moe_experts(xs, blk_e, nact, w_gate, w_up, w_down, layer):
    n_slots, d = xs.shape
    _, n_exp, _, hid = w_gate.shape
    bm = MOE_BLOCK_ROWS
    hb = _pick(hid, (MOE_HIDDEN_BLOCK, 128))
    nh = hid // hb

    def active(i, j, blk_e, nact):
        ib = jnp.minimum(i, nact[0] - 1)
        jb = jnp.where(i < nact[0], j, nh - 1)
        return ib, jb

    def xmap(i, j, blk_e, nact):
        return active(i, j, blk_e, nact)[0], 0

    def gmap(i, j, blk_e, nact):
        ib, jb = active(i, j, blk_e, nact)
        return layer, blk_e[ib], 0, jb

    def dmap(i, j, blk_e, nact):
        ib, jb = active(i, j, blk_e, nact)
        return layer, blk_e[ib], jb, 0

    return pl.pallas_call(
        _expert_kernel,
        grid_spec=pltpu.PrefetchScalarGridSpec(
            num_scalar_prefetch=2, grid=(n_slots // bm, nh),
            in_specs=[pl.BlockSpec((bm, d), xmap),
                      pl.BlockSpec((None, None, d, hb), gmap), pl.BlockSpec((None, None, d, hb), gmap),
                      pl.BlockSpec((None, None, hb, d), dmap)],
            out_specs=pl.BlockSpec((bm, d), xmap),
            scratch_shapes=[pltpu.VMEM((bm, d), BF16)]),
        out_shape=jax.ShapeDtypeStruct((n_slots, d), F32),
        input_output_aliases={2: 0},
        compiler_params=_cparams(("arbitrary", "arbitrary")),
    )(blk_e, nact, xs, w_gate, w_up, w_down)


def _combine_kernel(dest_ref, y_ref, gate_ref, w_ref, ys_ref, o_ref, buf, sem, *, tm, n_steps):
    i = pl.program_id(0)
    slot = lax.rem(i, 2)

    def row_copy(src_row, s, k, r):
        return pltpu.make_async_copy(ys_ref.at[pl.ds(src_row, 1), :], buf.at[s, k, pl.ds(r, 1), :], sem.at[s])

    def issue(step, s):
        def body(r, carry):
            a = 2 * (step * tm + r)
            row_copy(dest_ref[a], s, 0, r).start()
            row_copy(dest_ref[a + 1], s, 1, r).start()
            return carry
        lax.fori_loop(0, tm, body, 0)

    @pl.when(i == 0)
    def _():
        issue(0, 0)

    @pl.when(i + 1 < n_steps)
    def _():
        issue(i + 1, 1 - slot)

    def wait(r, carry):
        row_copy(0, slot, 0, 0).wait()
        row_copy(0, slot, 1, 0).wait()
        return carry
    lax.fori_loop(0, tm, wait, 0)

    w0 = w_ref[:, 0:1]
    w1 = w_ref[:, 1:2]
    o_ref[...] = y_ref[...] + gate_ref[...] * (w0 * buf[slot, 0] + w1 * buf[slot, 1])


def _moe_combine(dest, y, m6, rows, wts, ys):
    n, d = y.shape
    tm = _pick(np.gcd(rows.n_p, rows.ts), (128, 64, 32, 16, 8))
    grp = rows.group_of_block(tm)
    n_steps = n // tm
    return pl.pallas_call(
        functools.partial(_combine_kernel, tm=tm, n_steps=n_steps),
        grid_spec=pltpu.PrefetchScalarGridSpec(
            num_scalar_prefetch=1, grid=(n_steps,),
            in_specs=[pl.BlockSpec((tm, d), lambda i, dest: (i, 0)),
                      pl.BlockSpec((None, 1, d), lambda i, dest: (grp(i) * 6 + 5, 0, 0)),
                      pl.BlockSpec((tm, MOE_TOPK), lambda i, dest: (i, 0)),
                      pl.BlockSpec(memory_space=pl.ANY)],
            out_specs=pl.BlockSpec((tm, d), lambda i, dest: (i, 0)),
            scratch_shapes=[pltpu.VMEM((2, MOE_TOPK, tm, d), F32), pltpu.SemaphoreType.DMA((2,))]),
        out_shape=jax.ShapeDtypeStruct((n, d), F32),
        compiler_params=_cparams(("arbitrary",)),
    )(dest, y, m6, wts, ys)


def _moe_layer(y, gain, m6, rows, w_group, b_group, w_expert, b_expert, w_gate, w_up, w_down, layer):
    n, d = y.shape
    n_exp = w_expert.shape[1]
    n_grp = w_group.shape[1]
    assert n_exp + n_grp <= LANES and n_exp == n_grp * MOE_GROUP_EXPERTS
    pad = jnp.zeros((d, LANES - n_exp - n_grp), F32)
    wr = jnp.concatenate([w_expert, w_group, pad], axis=1)
    wr_hi, wr_lo = _split(wr)
    br = jnp.concatenate([b_expert, b_group, jnp.zeros((LANES - n_exp - n_grp,), F32)]).reshape(1, LANES)
    hp, route = _ffn_pre(y, gain, m6, rows, wr_hi, wr_lo, br, n_exp, n_grp)

    bm = MOE_BLOCK_ROWS
    n_assign = n * MOE_TOPK
    flat_e = route[:, :MOE_TOPK].astype(I32).reshape(n_assign)
    wts = route[:, MOE_TOPK:2 * MOE_TOPK]
    onehot = (flat_e[:, None] == jnp.arange(n_exp, dtype=I32)[None, :]).astype(I32)
    csum = jnp.cumsum(onehot, axis=0)
    counts = csum[-1]
    padded = (counts + bm - 1) // bm * bm
    pad_end = jnp.cumsum(padded)
    dest = jnp.sum(onehot * (csum - 1 + (pad_end - padded)[None, :]), axis=1)
    n_blocks = (n_assign + n_exp * (bm - 1) + bm - 1) // bm
    blk_start = jnp.arange(n_blocks, dtype=I32) * bm
    blk_e = jnp.minimum(jnp.sum((pad_end[None, :] <= blk_start[:, None]).astype(I32), axis=1), n_exp - 1)
    nact = (pad_end[-1:] // bm).astype(I32)

    xs = _moe_dispatch(dest, hp, n_blocks * bm)
    ys = _moe_experts(xs, blk_e, nact, w_gate, w_up, w_down, layer)
    return _moe_combine(dest, y, m6, rows, wts, ys)


def _rope_rows(t_len, rot_dim):
    rows = t_len // GRID_W
    row = jnp.repeat(jnp.arange(rows, dtype=F32), GRID_W)
    col = jnp.tile(jnp.arange(GRID_W, dtype=F32), rows)
    quarter = rot_dim // 4
    inv = ROPE_THETA ** (-jnp.arange(quarter, dtype=F32) / quarter)
    ang = jnp.concatenate([row[:, None] * inv, col[:, None] * inv], axis=-1)
    cos, sin = jnp.cos(ang), jnp.sin(ang)
    reps = LANES // rot_dim
    return (jnp.tile(jnp.concatenate([cos, cos], axis=-1), (1, reps)),
            jnp.tile(jnp.concatenate([-sin, sin], axis=-1), (1, reps)))


def _token_tables(rows, rot_dim):
    cos_s, sin_s = _rope_rows(rows.ts, rot_dim)
    cos = jnp.concatenate([jnp.ones((rows.n_p, LANES), F32)] + [cos_s] * rows.bs, axis=0)
    sin = jnp.concatenate([jnp.zeros((rows.n_p, LANES), F32)] + [sin_s] * rows.bs, axis=0)
    return cos, sin


def _kv_order(rows, tok, ctx):
    parts = []
    for b in range(rows.bs):
        parts += [ctx[b], tok[rows.n_p + b * rows.ts:rows.n_p + (b + 1) * rows.ts]]
    parts.append(tok[:rows.n_p])
    return jnp.concatenate(parts, axis=0)


def _pad_rows8(rows_list):
    a = jnp.stack(rows_list, axis=0)
    return jnp.concatenate([a, jnp.zeros((8 - a.shape[0], a.shape[1]), a.dtype)], axis=0)


def _ab_layer(y, h, m6, rows, past, p, ctx):
    n, d = y.shape
    q_rank = p['q_norm'].shape[0]
    kv_rank = p['kv_norm'].shape[0]
    heads = p['w_uq'].shape[1] // (MLA_NOPE + MLA_ROPE)
    mla_in = q_rank + kv_rank + MLA_ROPE
    half = MLA_ROPE // 2
    swap = np.concatenate([np.arange(half, MLA_ROPE), np.arange(0, half)])

    w_in = p['w_in']
    w_kr = w_in[:, q_rank + kv_rank:mla_in]
    w_mla = jnp.concatenate([w_in[:, :q_rank + kv_rank], w_kr, w_kr, w_kr[:, swap], w_kr[:, swap]], axis=1).astype(BF16)
    zm = _matmul([h], [w_mla])
    z_rwkv = _matmul([h], [w_in[:, mla_in:].astype(BF16)])

    cqn, ckvn, ckvn_b = _mla_mid(zm, p['q_norm'], p['kv_norm'], q_rank, kv_rank)
    w_uq = p['w_uq'].reshape(q_rank, heads, MLA_NOPE + MLA_ROPE)
    w_rope = w_uq[:, :, MLA_NOPE:]
    w_q = jnp.concatenate([w_uq[:, :, :MLA_NOPE].reshape(q_rank, -1), w_rope.reshape(q_rank, -1),
                           w_rope[:, :, swap].reshape(q_rank, -1)], axis=1).astype(BF16)
    zq = _matmul([cqn], [w_q])
    cos_t, sin_t = _token_tables(rows, MLA_ROPE)

    def gains(g):
        g_r = g[MLA_NOPE:]
        return _pad_rows8([g[:MLA_NOPE], jnp.concatenate([g_r, g_r]), jnp.concatenate([g_r[swap], g_r[swap]])])

    scale = float(MLA_NOPE + MLA_ROPE) ** -0.5
    qn, qr = _mla_q_prep(zq, gains(p['qk_norm'][0]), cos_t, sin_t, heads, scale)

    ctx_ckv, ctx_kr, s_f0, s_b0 = ctx
    w_ukv = p['w_ukv'].reshape(kv_rank, heads, MLA_NOPE + MLA_V)
    w_kv = jnp.concatenate([w_ukv[:, :, :MLA_NOPE].reshape(kv_rank, -1), w_ukv[:, :, MLA_NOPE:].reshape(kv_rank, -1)],
                           axis=1).astype(BF16)
    zkv = _matmul([_kv_order(rows, ckvn_b, ctx_ckv.astype(BF16))], [w_kv])
    kr_off = q_rank + kv_rank
    ctx_kr2 = jnp.concatenate([ctx_kr, ctx_kr], axis=-1)
    kr2 = _kv_order(rows, zm[:, kr_off:kr_off + LANES], ctx_kr2)
    krs2 = _kv_order(rows, zm[:, kr_off + LANES:kr_off + 2 * LANES], ctx_kr2)
    one_c = jnp.ones((rows.bs, past, LANES), F32)
    cos_kv = _kv_order(rows, cos_t, one_c)
    sin_kv = _kv_order(rows, sin_t, 0.0 * one_c)
    kn, krh, vv = _mla_k_prep(zkv, kr2, krs2, gains(p['qk_norm'][1]), cos_kv, sin_kv, heads)

    o_mla = jnp.zeros((n, heads * MLA_V), BF16)
    o_mla = _mla_attn(qn, qr, kn, krh, vv, o_mla, heads, rows.bp, rows.tp, rows.tp, 0, rows.bs * (past + rows.ts))
    o_mla = _mla_attn(qn, qr, kn, krh, vv, o_mla, heads, rows.bs, rows.ts, past + rows.ts, rows.n_p, 0)

    par = _rwkv_params(p)
    c = par['w0'].shape[1]
    o_rwkv = jnp.zeros((n, c), BF16)
    o_rwkv, sf, sb = _rwkv(z_rwkv, o_rwkv, par, rows.bp, rows.tp, 0)
    s0 = (_states_to_blockdiag(s_f0), _states_to_blockdiag(s_b0))
    o_rwkv, _, _ = _rwkv(z_rwkv, o_rwkv, par, rows.bs, rows.ts, rows.n_p, s0)

    hm = heads * MLA_V
    w_out = p['w_out'].astype(BF16)
    y = _matmul([o_mla, o_rwkv], [w_out[:hm], w_out[hm:]], res=y, m6=m6, gate_idx=2, rows=rows)
    new_ckv = ckvn[:rows.n_p].reshape(rows.bp, rows.tp, kv_rank)
    new_kr = zm[:rows.n_p, kr_off:kr_off + MLA_ROPE].reshape(rows.bp, rows.tp, MLA_ROPE)
    return y, (new_ckv, new_kr, _blockdiag_to_states(sf), _blockdiag_to_states(sb))


def _gqa_layer(y, h, m6, rows, past, p, ctx):
    n, d = y.shape
    heads = p['w_out'].shape[0] // ATTN_HEAD
    kv_heads = heads // ATTN_GROUP
    nq, nk = heads * ATTN_HEAD, kv_heads * ATTN_HEAD
    z = _matmul([h], [p['w_in'].astype(BF16)])
    cos_t, sin_t = _token_tables(rows, ATTN_HEAD)
    g2 = _pad_rows8([p['qk_norm'][0], p['qk_norm'][1]])
    q, k_keep, k_rot, v_b = _gqa_prep(z, g2, cos_t, sin_t, heads, kv_heads, float(ATTN_HEAD) ** -0.5)
    ctx_k, ctx_v = ctx
    k_all = _kv_order(rows, k_rot, ctx_k.reshape(rows.bs, past, nk).astype(BF16))
    v_all = _kv_order(rows, v_b, ctx_v.reshape(rows.bs, past, nk).astype(BF16))
    o = jnp.zeros((n, nq), BF16)
    o = _gqa_attn(q, k_all, v_all, o, kv_heads, rows.bp, rows.tp, rows.tp, 0, rows.bs * (past + rows.ts))
    o = _gqa_attn(q, k_all, v_all, o, kv_heads, rows.bs, rows.ts, past + rows.ts, rows.n_p, 0)
    y = _matmul([o], [p['w_out'].astype(BF16)], res=y, m6=m6, gate_idx=2, rows=rows)
    new_k = k_keep[:rows.n_p].reshape(rows.bp, rows.tp, kv_heads, ATTN_HEAD)
    new_v = z[:rows.n_p, nq + nk:].reshape(rows.bp, rows.tp, kv_heads, ATTN_HEAD)
    return y, (new_k, new_v)


def kernel(x_prompt, x_sample, c, c_ctx, cache_mla_ckv, cache_mla_krope, state_rwkv_fwd, state_rwkv_bwd, cache_attn_k, cache_attn_v, mod_w, mod_b, norm_mix, norm_ffn, ab_w_in, ab_w_out, mla_q_norm, mla_w_uq, mla_kv_norm, mla_w_ukv, mla_qk_norm, rwkv_mu, rwkv_w0, rwkv_w2, rwkv_a0, rwkv_a2, rwkv_g2, rwkv_k_k, rwkv_k_a, rwkv_r_k, rwkv_ln, gqa_w_in, gqa_qk_norm, gqa_w_out, moe_w_group, moe_b_group, moe_w_expert, moe_b_expert, moe_w_gate, moe_w_up, moe_w_down):
    bp, tp, d = x_prompt.shape
    bs, ts, _ = x_sample.shape
    past = cache_mla_ckv.shape[2]
    depth = mod_w.shape[0]
    rows = _Rows(bp, tp, bs, ts)
    assert bs + 1 <= 8

    cond8 = jnp.concatenate([c_ctx[None, :], c, jnp.zeros((8 - 1 - bs, d), F32)], axis=0)
    m_all = _adaln_all(cond8, mod_w, mod_b)
    y = jnp.concatenate([x_prompt.reshape(bp * tp, d), x_sample.reshape(bs * ts, d)], axis=0)

    ab_out, gqa_out = [], []
    for layer in range(depth):
        m6 = m_all[layer].reshape(8 * 6, 1, d)
        h = _norm_mod(y, norm_mix[layer], m6, rows, 0, 1)
        i = layer // 2
        if layer % 2 == 0:
            p = dict(w_in=ab_w_in[i], w_out=ab_w_out[i], q_norm=mla_q_norm[i], w_uq=mla_w_uq[i],
                     kv_norm=mla_kv_norm[i], w_ukv=mla_w_ukv[i], qk_norm=mla_qk_norm[i], mu=rwkv_mu[i],
                     w0=rwkv_w0[i], w2=rwkv_w2[i], a0=rwkv_a0[i], a2=rwkv_a2[i], g2=rwkv_g2[i],
                     k_k=rwkv_k_k[i], k_a=rwkv_k_a[i], r_k=rwkv_r_k[i], ln=rwkv_ln[i])
            y, new = _ab_layer(y, h, m6, rows, past, p, (cache_mla_ckv[:, i], cache_mla_krope[:, i],
                                                          state_rwkv_fwd[:, i], state_rwkv_bwd[:, i]))
            ab_out.append(new)
        else:
            p = dict(w_in=gqa_w_in[i], qk_norm=gqa_qk_norm[i], w_out=gqa_w_out[i])
            y, new = _gqa_layer(y, h, m6, rows, past, p, (cache_attn_k[:, i], cache_attn_v[:, i]))
            gqa_out.append(new)
        y = _moe_layer(y, norm_ffn[layer], m6, rows, moe_w_group[layer], moe_b_group[layer], moe_w_expert[layer],
                       moe_b_expert[layer], moe_w_gate, moe_w_up, moe_w_down, layer)

    yp = y[:rows.n_p].reshape(bp, tp, d)
    ys = y[rows.n_p:].reshape(bs, ts, d)
    stack = lambda outs, k: jnp.stack([o[k] for o in outs], axis=1)
    return (yp, ys, stack(ab_out, 0), stack(ab_out, 1), stack(ab_out, 2), stack(ab_out, 3),
            stack(gqa_out, 0), stack(gqa_out, 1))
```

```python
import functools

import numpy as np
import jax
import jax.numpy as jnp
from jax import lax
from jax.experimental import pallas as pl
from jax.experimental.pallas import tpu as pltpu

F32 = jnp.float32
BF16 = jnp.bfloat16
I32 = jnp.int32

GRID_W = 64
ROPE_THETA = 10000.0
NORM_EPS = 1e-6
GN_EPS = 64e-5
MLA_NOPE = 128
MLA_ROPE = 64
MLA_V = 128
RWKV_HEAD = 64
ATTN_HEAD = 128
ATTN_GROUP = 4
MOE_TOPK = 2
MOE_GROUP_EXPERTS = 8

LANES = 128
RWKV_CHUNK = 64
RWKV_UNROLL = 8
MOE_BLOCK_ROWS = 384
MOE_HIDDEN_BLOCK = 256
ATTN_KV_BLOCK_BYTES = 4 * 1024 * 1024
VMEM_LIMIT = 56 * 1024 * 1024
NEG_BIG = -1e30


def _pick(n, prefs):
    for p in prefs:
        if n % p == 0:
            return p
    return n


def _cparams(sem):
    return pltpu.CompilerParams(dimension_semantics=sem, vmem_limit_bytes=VMEM_LIMIT)


def _dot(a, b):
    return jnp.dot(a, b, preferred_element_type=F32)


def _dot_nt(a, b):
    return lax.dot_general(a, b, (((1,), (1,)), ((), ())), preferred_element_type=F32)


def _dot_tn(a, b):
    return lax.dot_general(a, b, (((0,), (0,)), ((), ())), preferred_element_type=F32)


def _split(x):
    hi = x.astype(BF16)
    lo = (x - hi.astype(F32)).astype(BF16)
    return hi, lo


def _dot3(a, b):
    ah, al = _split(a)
    bh, bl = _split(b)
    return _dot(ah, bh) + _dot(ah, bl) + _dot(al, bh)


def _mod_kernel(c_ref, w_ref, b_ref, o_ref):
    c = c_ref[...]
    s = c * jax.nn.sigmoid(c)
    sh, sl = _split(s)
    w = w_ref[...].astype(BF16)
    o_ref[...] = _dot(sh, w) + _dot(sl, w) + b_ref[...]


def _adaln_all(cond8, mod_w, mod_b):
    depth, d, n6 = mod_w.shape
    tn = _pick(n6, (512, 256, 128))
    return pl.pallas_call(
        _mod_kernel,
        grid=(depth, n6 // tn),
        in_specs=[pl.BlockSpec((8, d), lambda l, j: (0, 0)),
                  pl.BlockSpec((None, d, tn), lambda l, j: (l, 0, j)),
                  pl.BlockSpec((None, 1, tn), lambda l, j: (l, 0, j))],
        out_specs=pl.BlockSpec((None, 8, tn), lambda l, j: (l, 0, j)),
        out_shape=jax.ShapeDtypeStruct((depth, 8, n6), F32),
        compiler_params=_cparams(("arbitrary", "arbitrary")),
    )(cond8, mod_w, mod_b.reshape(depth, 1, n6))


class _Rows:
    def __init__(self, bp, tp, bs, ts):
        self.bp, self.tp, self.bs, self.ts = bp, tp, bs, ts
        self.n_p = bp * tp
        self.n = bp * tp + bs * ts

    def group_of_block(self, tm):
        n_p, ts = self.n_p, self.ts
        assert n_p % tm == 0 and ts % tm == 0
        return lambda i: jnp.where(i * tm < n_p, 0, 1 + (i * tm - n_p) // ts)


def _norm_mod_kernel(y_ref, g_ref, sh_ref, sc_ref, o_ref):
    y = y_ref[...]
    ms = jnp.mean(y * y, axis=-1, keepdims=True)
    h = y * lax.rsqrt(ms + NORM_EPS) * g_ref[...]
    h = h * (1.0 + sc_ref[...]) + sh_ref[...]
    o_ref[...] = h.astype(o_ref.dtype)


def _norm_mod(y, gain, m6, rows, shift_idx, scale_idx):
    n, d = y.shape
    tm = _pick(rows.tp, (256, 128, 64, 32, 16, 8))
    tm = tm if rows.ts % tm == 0 else _pick(np.gcd(rows.tp, rows.ts), (256, 128, 64, 32, 16, 8))
    grp = rows.group_of_block(tm)
    return pl.pallas_call(
        _norm_mod_kernel,
        grid=(n // tm,),
        in_specs=[pl.BlockSpec((tm, d), lambda i: (i, 0)),
                  pl.BlockSpec((1, d), lambda i: (0, 0)),
                  pl.BlockSpec((None, 1, d), lambda i: (grp(i) * 6 + shift_idx, 0, 0)),
                  pl.BlockSpec((None, 1, d), lambda i: (grp(i) * 6 + scale_idx, 0, 0))],
        out_specs=pl.BlockSpec((tm, d), lambda i: (i, 0)),
        out_shape=jax.ShapeDtypeStruct((n, d), BF16),
        compiler_params=_cparams(("arbitrary",)),
    )(y, gain.reshape(1, d), m6, m6)


def _mm_kernel(*refs, n_in, has_res):
    o_ref = refs[-1]
    acc = None
    for i in range(n_in):
        w = refs[n_in + i][...]
        if w.dtype != BF16:
            w = w.astype(BF16)
        part = _dot(refs[i][...], w)
        acc = part if acc is None else acc + part
    if has_res:
        acc = refs[2 * n_in][...] + refs[2 * n_in + 1][...] * acc
    o_ref[...] = acc.astype(o_ref.dtype)


def _matmul(xs, ws, out_dtype=F32, res=None, m6=None, gate_idx=None, rows=None, tm_prefs=(1024, 512, 256, 128, 64, 32, 16, 8)):
    m = xs[0].shape[0]
    n = ws[0].shape[1]
    if res is not None:
        cands = [t for t in tm_prefs if rows.n_p % t == 0 and rows.ts % t == 0]
        tm = cands[0]
    else:
        tm = _pick(m, tm_prefs)
    tn = _pick(n, (512, 384, 256, 128))
    in_specs = [pl.BlockSpec((tm, x.shape[1]), lambda i, j: (i, 0)) for x in xs]
    in_specs += [pl.BlockSpec((w.shape[0], tn), lambda i, j: (0, j)) for w in ws]
    args = list(xs) + list(ws)
    if res is not None:
        grp = rows.group_of_block(tm)
        in_specs += [pl.BlockSpec((tm, tn), lambda i, j: (i, j)),
                     pl.BlockSpec((None, 1, tn), lambda i, j: (grp(i) * 6 + gate_idx, 0, j))]
        args += [res, m6]
    return pl.pallas_call(
        functools.partial(_mm_kernel, n_in=len(xs), has_res=res is not None),
        grid=(m // tm, n // tn),
        in_specs=in_specs,
        out_specs=pl.BlockSpec((tm, tn), lambda i, j: (i, j)),
        out_shape=jax.ShapeDtypeStruct((m, n), out_dtype),
        compiler_params=_cparams(("arbitrary", "arbitrary")),
    )(*args)


def _mla_mid_kernel(z_ref, gq_ref, gkv_ref, cq_ref, ckv_ref, ckvb_ref, *, q_rank, kv_rank):
    cq = z_ref[:, :q_rank]
    ms = jnp.mean(cq * cq, axis=-1, keepdims=True)
    cq_ref[...] = (cq * lax.rsqrt(ms + NORM_EPS) * gq_ref[...]).astype(BF16)
    ckv = z_ref[:, q_rank:q_rank + kv_rank]
    ms = jnp.mean(ckv * ckv, axis=-1, keepdims=True)
    ckvn = ckv * lax.rsqrt(ms + NORM_EPS) * gkv_ref[...]
    ckv_ref[...] = ckvn
    ckvb_ref[...] = ckvn.astype(BF16)


def _mla_mid(zm, q_norm, kv_norm, q_rank, kv_rank):
    n, w = zm.shape
    tm = _pick(n, (512, 256, 128, 64, 32, 16, 8))
    return pl.pallas_call(
        functools.partial(_mla_mid_kernel, q_rank=q_rank, kv_rank=kv_rank),
        grid=(n // tm,),
        in_specs=[pl.BlockSpec((tm, w), lambda i: (i, 0)),
                  pl.BlockSpec((1, q_rank), lambda i: (0, 0)),
                  pl.BlockSpec((1, kv_rank), lambda i: (0, 0))],
        out_specs=[pl.BlockSpec((tm, q_rank), lambda i: (i, 0)),
                   pl.BlockSpec((tm, kv_rank), lambda i: (i, 0)),
                   pl.BlockSpec((tm, kv_rank), lambda i: (i, 0))],
        out_shape=[jax.ShapeDtypeStruct((n, q_rank), BF16),
                   jax.ShapeDtypeStruct((n, kv_rank), F32),
                   jax.ShapeDtypeStruct((n, kv_rank), BF16)],
        compiler_params=_cparams(("arbitrary",)),
    )(zm, q_norm.reshape(1, q_rank), kv_norm.reshape(1, kv_rank))


def _low_half_mask(shape):
    return lax.broadcasted_iota(I32, shape, 1) < (LANES // 2)


def _mla_q_prep_kernel(z_ref, g_ref, cos_ref, sin_ref, qn_ref, qr_ref, *, heads, scale):
    hn = heads * MLA_NOPE
    hr = heads * MLA_ROPE
    qk = float(MLA_NOPE + MLA_ROPE)
    lo = _low_half_mask(cos_ref.shape)
    cos = cos_ref[...]
    sin = sin_ref[...]
    g_n, g_r, g_s = g_ref[0:1, :], g_ref[1:2, :], g_ref[2:3, :]
    for p in range(heads // 2):
        rope = z_ref[:, hn + LANES * p:hn + LANES * (p + 1)]
        rsw = z_ref[:, hn + hr + LANES * p:hn + hr + LANES * (p + 1)]
        r2 = rope * rope
        ss_r = (jnp.sum(jnp.where(lo, r2, 0.0), axis=-1, keepdims=True),
                jnp.sum(jnp.where(lo, 0.0, r2), axis=-1, keepdims=True))
        inv = []
        for hh in range(2):
            h = 2 * p + hh
            nope = z_ref[:, MLA_NOPE * h:MLA_NOPE * (h + 1)]
            ss = jnp.sum(nope * nope, axis=-1, keepdims=True) + ss_r[hh]
            inv_h = lax.rsqrt(ss / qk + NORM_EPS) * scale
            inv.append(inv_h)
            qn_ref[:, MLA_NOPE * h:MLA_NOPE * (h + 1)] = (nope * inv_h * g_n).astype(BF16)
        inv2 = jnp.where(lo, inv[0], inv[1])
        qr_ref[:, LANES * p:LANES * (p + 1)] = ((rope * g_r * cos + rsw * g_s * sin) * inv2).astype(BF16)


def _mla_q_prep(zq, g3, cos, sin, heads, scale):
    n, w = zq.shape
    tm = _pick(n, (256, 128, 64, 32, 16, 8))
    return pl.pallas_call(
        functools.partial(_mla_q_prep_kernel, heads=heads, scale=scale),
        grid=(n // tm,),
        in_specs=[pl.BlockSpec((tm, w), lambda i: (i, 0)),
                  pl.BlockSpec((8, LANES), lambda i: (0, 0)),
                  pl.BlockSpec((tm, LANES), lambda i: (i, 0)),
                  pl.BlockSpec((tm, LANES), lambda i: (i, 0))],
        out_specs=[pl.BlockSpec((tm, heads * MLA_NOPE), lambda i: (i, 0)),
                   pl.BlockSpec((tm, heads * MLA_ROPE), lambda i: (i, 0))],
        out_shape=[jax.ShapeDtypeStruct((n, heads * MLA_NOPE), BF16),
                   jax.ShapeDtypeStruct((n, heads * MLA_ROPE), BF16)],
        compiler_params=_cparams(("arbitrary",)),
    )(zq, g3, cos, sin)


def _mla_k_prep_kernel(z_ref, kr_ref, krs_ref, g_ref, cos_ref, sin_ref, kn_ref, krh_ref, v_ref, *, heads):
    hn = heads * MLA_NOPE
    qk = float(MLA_NOPE + MLA_ROPE)
    lo = _low_half_mask(cos_ref.shape)
    g_n, g_r, g_s = g_ref[0:1, :], g_ref[1:2, :], g_ref[2:3, :]
    kr = kr_ref[...]
    ss_r = jnp.sum(jnp.where(lo, kr * kr, 0.0), axis=-1, keepdims=True)
    base = kr * g_r * cos_ref[...] + krs_ref[...] * g_s * sin_ref[...]
    for p in range(heads // 2):
        inv = []
        for hh in range(2):
            h = 2 * p + hh
            nope = z_ref[:, MLA_NOPE * h:MLA_NOPE * (h + 1)]
            ss = jnp.sum(nope * nope, axis=-1, keepdims=True) + ss_r
            inv_h = lax.rsqrt(ss / qk + NORM_EPS)
            inv.append(inv_h)
            kn_ref[:, MLA_NOPE * h:MLA_NOPE * (h + 1)] = (nope * inv_h * g_n).astype(BF16)
        krh_ref[:, LANES * p:LANES * (p + 1)] = (base * jnp.where(lo, inv[0], inv[1])).astype(BF16)
    v_ref[...] = z_ref[:, hn:].astype(BF16)


def _mla_k_prep(zkv, kr2, krs2, g3, cos, sin, heads):
    n, w = zkv.shape
    tm = _pick(n, (256, 128, 64, 32, 16, 8))
    hn = heads * MLA_NOPE
    row = lambda i: (i, 0)
    return pl.pallas_call(
        functools.partial(_mla_k_prep_kernel, heads=heads),
        grid=(n // tm,),
        in_specs=[pl.BlockSpec((tm, w), row), pl.BlockSpec((tm, LANES), row), pl.BlockSpec((tm, LANES), row),
                  pl.BlockSpec((8, LANES), lambda i: (0, 0)),
                  pl.BlockSpec((tm, LANES), row), pl.BlockSpec((tm, LANES), row)],
        out_specs=[pl.BlockSpec((tm, hn), row), pl.BlockSpec((tm, heads * MLA_ROPE), row),
                   pl.BlockSpec((tm, hn), row)],
        out_shape=[jax.ShapeDtypeStruct((n, hn), BF16), jax.ShapeDtypeStruct((n, heads * MLA_ROPE), BF16),
                   jax.ShapeDtypeStruct((n, hn), BF16)],
        compiler_params=_cparams(("arbitrary",)),
    )(zkv, kr2, krs2, g3, cos, sin)


def _softmax_pv(s, v):
    m = jnp.max(s, axis=-1, keepdims=True)
    p = jnp.exp(s - m)
    l = jnp.sum(p, axis=-1, keepdims=True)
    return _dot(p.astype(BF16), v) / l


def _mla_attn_kernel(qn_ref, qr_ref, kn_ref, kr_ref, v_ref, _aliased_out, o_ref, *, pairs):
    lo = _low_half_mask((qr_ref.shape[0], LANES))
    for p in range(pairs):
        qr = qr_ref[:, LANES * p:LANES * (p + 1)]
        kr = kr_ref[:, LANES * p:LANES * (p + 1)]
        zero = jnp.zeros_like(qr)
        for hh in range(2):
            h = 2 * p + hh
            sl = slice(MLA_NOPE * h, MLA_NOPE * (h + 1))
            qr_h = jnp.where(lo, qr, zero) if hh == 0 else jnp.where(lo, zero, qr)
            s = _dot_nt(qn_ref[:, sl], kn_ref[:, sl]) + _dot_nt(qr_h, kr)
            o_ref[:, sl] = _softmax_pv(s, v_ref[:, sl]).astype(o_ref.dtype)


def _mla_attn(qn, qr, kn, krh, v, out, heads, nb, tq_len, tk_len, q_row0, kv_row0):
    tq = _pick(tq_len, (256, 128, 64, 32, 16, 8))
    nq = tq_len // tq
    assert q_row0 % tq == 0 and kv_row0 % tk_len == 0
    qb0, kb0 = q_row0 // tq, kv_row0 // tk_len
    pairs = max(g for g in range(1, heads // 2 + 1)
                if (heads // 2) % g == 0 and (g == 1 or tk_len * g * 5 * LANES * 2 <= ATTN_KV_BLOCK_BYTES))
    qmap = lambda b, p, i: (qb0 + b * nq + i, p)
    kmap = lambda b, p, i: (kb0 + b, p)
    return pl.pallas_call(
        functools.partial(_mla_attn_kernel, pairs=pairs),
        grid=(nb, heads // 2 // pairs, nq),
        in_specs=[pl.BlockSpec((tq, 2 * MLA_NOPE * pairs), qmap), pl.BlockSpec((tq, LANES * pairs), qmap),
                  pl.BlockSpec((tk_len, 2 * MLA_NOPE * pairs), kmap), pl.BlockSpec((tk_len, LANES * pairs), kmap),
                  pl.BlockSpec((tk_len, 2 * MLA_V * pairs), kmap),
                  pl.BlockSpec(memory_space=pl.ANY)],
        out_specs=pl.BlockSpec((tq, 2 * MLA_V * pairs), qmap),
        out_shape=jax.ShapeDtypeStruct(out.shape, out.dtype),
        input_output_aliases={5: 0},
        compiler_params=_cparams(("arbitrary", "arbitrary", "arbitrary")),
    )(qn, qr, kn, krh, v, out)


def _gqa_prep_kernel(z_ref, g_ref, cos_ref, sin_ref, q_ref, kk_ref, kr_ref, v_ref, *, heads, kv_heads, scale):
    cos = cos_ref[...]
    sin = sin_ref[...]
    g_q, g_k = g_ref[0:1, :], g_ref[1:2, :]

    def normed(x, g):
        ms = jnp.mean(x * x, axis=-1, keepdims=True)
        return x * lax.rsqrt(ms + NORM_EPS) * g

    def rope(x):
        return x * cos + pltpu.roll(x, ATTN_HEAD // 2, 1) * sin

    for h in range(heads):
        sl = slice(ATTN_HEAD * h, ATTN_HEAD * (h + 1))
        q_ref[:, sl] = (rope(normed(z_ref[:, sl], g_q)) * scale).astype(BF16)
    for h in range(kv_heads):
        sl = slice(ATTN_HEAD * h, ATTN_HEAD * (h + 1))
        kn = normed(z_ref[:, ATTN_HEAD * (heads + h):ATTN_HEAD * (heads + h + 1)], g_k)
        kk_ref[:, sl] = kn
        kr_ref[:, sl] = rope(kn).astype(BF16)
    v_ref[...] = z_ref[:, ATTN_HEAD * (heads + kv_heads):].astype(BF16)


def _gqa_prep(z, g2, cos, sin, heads, kv_heads, scale):
    n, w = z.shape
    tm = _pick(n, (256, 128, 64, 32, 16, 8))
    row = lambda i: (i, 0)
    nq, nk = heads * ATTN_HEAD, kv_heads * ATTN_HEAD
    return pl.pallas_call(
        functools.partial(_gqa_prep_kernel, heads=heads, kv_heads=kv_heads, scale=scale),
        grid=(n // tm,),
        in_specs=[pl.BlockSpec((tm, w), row), pl.BlockSpec((8, LANES), lambda i: (0, 0)),
                  pl.BlockSpec((tm, LANES), row), pl.BlockSpec((tm, LANES), row)],
        out_specs=[pl.BlockSpec((tm, nq), row), pl.BlockSpec((tm, nk), row),
                   pl.BlockSpec((tm, nk), row), pl.BlockSpec((tm, nk), row)],
        out_shape=[jax.ShapeDtypeStruct((n, nq), BF16), jax.ShapeDtypeStruct((n, nk), F32),
                   jax.ShapeDtypeStruct((n, nk), BF16), jax.ShapeDtypeStruct((n, nk), BF16)],
        compiler_params=_cparams(("arbitrary",)),
    )(z, g2, cos, sin)


def _gqa_attn_kernel(q_ref, k_ref, v_ref, _aliased_out, o_ref, *, kv_per_step):
    for j in range(kv_per_step):
        k = k_ref[:, ATTN_HEAD * j:ATTN_HEAD * (j + 1)]
        v = v_ref[:, ATTN_HEAD * j:ATTN_HEAD * (j + 1)]
        for g in range(ATTN_GROUP):
            h = j * ATTN_GROUP + g
            sl = slice(ATTN_HEAD * h, ATTN_HEAD * (h + 1))
            o_ref[:, sl] = _softmax_pv(_dot_nt(q_ref[:, sl], k), v).astype(o_ref.dtype)


def _gqa_attn(q, k, v, out, kv_heads, nb, tq_len, tk_len, q_row0, kv_row0):
    tq = _pick(tq_len, (256, 128, 64, 32, 16, 8))
    nq = tq_len // tq
    assert q_row0 % tq == 0 and kv_row0 % tk_len == 0
    qb0, kb0 = q_row0 // tq, kv_row0 // tk_len
    per = max(g for g in range(1, kv_heads + 1)
              if kv_heads % g == 0 and (g == 1 or tk_len * g * 2 * ATTN_HEAD * 2 <= ATTN_KV_BLOCK_BYTES))
    qmap = lambda b, h, i: (qb0 + b * nq + i, h)
    kmap = lambda b, h, i: (kb0 + b, h)
    gw = ATTN_GROUP * ATTN_HEAD * per
    return pl.pallas_call(
        functools.partial(_gqa_attn_kernel, kv_per_step=per),
        grid=(nb, kv_heads // per, nq),
        in_specs=[pl.BlockSpec((tq, gw), qmap), pl.BlockSpec((tk_len, ATTN_HEAD * per), kmap),
                  pl.BlockSpec((tk_len, ATTN_HEAD * per), kmap), pl.BlockSpec(memory_space=pl.ANY)],
        out_specs=pl.BlockSpec((tq, gw), qmap),
        out_shape=jax.ShapeDtypeStruct(out.shape, out.dtype),
        input_output_aliases={3: 0},
        compiler_params=_cparams(("arbitrary", "arbitrary", "arbitrary")),
    )(q, k, v, out)


def _seg_sum(x, lo):
    s0 = jnp.sum(jnp.where(lo, x, 0.0), axis=-1, keepdims=True)
    s1 = jnp.sum(jnp.where(lo, 0.0, x), axis=-1, keepdims=True)
    return jnp.where(lo, s0, s1)


def _centred_shift(z, mu_ref):
    t = z.shape[0]
    ti = lax.broadcasted_iota(I32, z.shape, 0)
    prev = jnp.where(ti == 0, 0.0, pltpu.roll(z, 1, 0))
    nxt = jnp.where(ti == t - 1, 0.0, pltpu.roll(z, t - 1, 0))
    return z + mu_ref[0:1, :] * (prev - z) + mu_ref[1:2, :] * (nxt - z)


def _stack_heads(x, lo):
    return jnp.concatenate([jnp.where(lo, x, 0.0), jnp.where(lo, 0.0, x)], axis=0)


def _rwkv_chunks(units, r_s, v_s, kk_s, lw_s, kh_s, bb_s, consts):
    L = RWKV_CHUNK
    U = range(len(units))
    cst = [consts[d] for d, _ in units]
    rows = [pl.ds(c0, L) for _, c0 in units]
    lo, eye = cst[0][4], cst[0][3]
    lw = [lw_s[units[u][0], rows[u], :] for u in U]
    hi = [x.astype(BF16) for x in lw]
    r1 = [lw[u] - hi[u].astype(F32) for u in U]
    mid = [x.astype(BF16) for x in r1]
    low = [(r1[u] - mid[u].astype(F32)).astype(BF16) for u in U]
    cum = [_dot(cst[u][0], hi[u]) + _dot(cst[u][0], mid[u]) + _dot(cst[u][0], low[u]) for u in U]
    g_inc = [jnp.exp(x) for x in cum]
    g_inv = [jnp.exp(-x) for x in cum]
    kap = [_stack_heads(kk_s[rows[u], :] * jnp.exp(cum[u] - lw[u]), lo).astype(BF16) for u in U]
    g_end = [g_inc[u][L - 1:L, :] if units[u][0] == 0 else g_inc[u][0:1, :] for u in U]
    bt32 = [_stack_heads(bb_s[units[u][0], rows[u], :] * g_inv[u], lo) for u in U]
    kt32 = [_stack_heads(kh_s[units[u][0], rows[u], :] * g_inv[u], lo) for u in U]
    rt32 = [_stack_heads(r_s[rows[u], :] * g_inc[u], lo) for u in U]
    bt = [x.astype(BF16) for x in bt32]
    kt = [x.astype(BF16) for x in kt32]
    rt = [x.astype(BF16) for x in rt32]
    vr = [_stack_heads(v_s[rows[u], :], lo).astype(BF16) for u in U]
    n2 = 2 * L
    gram = [_dot_nt(jnp.concatenate([kap[u], rt[u]], axis=0), jnp.concatenate([bt[u], kt[u]], axis=0)) for u in U]
    lb = [jnp.where(cst[u][1], gram[u][:n2, :n2], 0.0) for u in U]
    lk = [jnp.where(cst[u][1], gram[u][:n2, n2:], 0.0).astype(BF16) for u in U]
    hb = [jnp.where(cst[u][2], gram[u][n2:, :n2], 0.0).astype(BF16) for u in U]
    hk = [jnp.where(cst[u][2], gram[u][n2:, n2:], 0.0).astype(BF16) for u in U]
    x = [jnp.where(eye, 1.0, 0.0) - lb[u] for u in U]
    lbb = [y.astype(BF16) for y in lb]
    p = [_dot(y, y) for y in lbb]
    lkv = [_dot(lk[u], vr[u]).astype(BF16) for u in U]
    n_iter = int(np.log2(L)) - 1
    for it in range(n_iter):
        pb = [y.astype(BF16) for y in p]
        x = [x[u] + _dot(x[u].astype(BF16), pb[u]) for u in U]
        if it + 1 < n_iter:
            p = [_dot(y, y) for y in pb]
    xb = [y.astype(BF16) for y in x]
    a12 = [(-_dot(xb[u], jnp.concatenate([kap[u], lkv[u]], axis=1))).astype(BF16) for u in U]
    btg = [(bt32[u] * g_end[u]).astype(BF16) for u in U]
    ktg = [(kt32[u] * g_end[u]).astype(BF16) for u in U]
    mn = [_dot_tn(btg[u], a12[u]) for u in U]
    ha = [_dot(hb[u], a12[u]) for u in U]
    m = [jnp.where(eye, jnp.broadcast_to(g_end[u], eye.shape), 0.0) + mn[u][:, :n2] for u in U]
    nn = [mn[u][:, n2:] + _dot_tn(ktg[u], vr[u]) for u in U]
    y1 = [rt32[u] + ha[u][:, :n2] for u in U]
    y2 = [ha[u][:, n2:] + _dot(hk[u], vr[u]) for u in U]
    return [(m[u].astype(BF16), nn[u], y1[u].astype(BF16), y2[u]) for u in U]


def _rwkv_apply(st, local):
    m, nn, y1, y2 = local
    L = RWKV_CHUNK
    stb = st.astype(BF16)
    ysm = _dot(y1, stb) + y2
    return ysm[:L, :] + ysm[L:, :], _dot(m, stb) + nn


def _rwkv_kernel(*refs, t_len, has_s0):
    (zr_ref, zk_ref, zv_ref, zs_ref, mur_ref, muk_ref, muv_ref, mus_ref, w0_ref, w2_ref, a0_ref, a2_ref,
     g2_ref, kkg_ref, ka_ref, rk_ref, ln_ref) = refs[:17]
    pos = 17
    if has_s0:
        s0f_ref, s0b_ref = refs[17:19]
        pos = 19
    o_ref, sf_ref, sb_ref = refs[pos:pos + 3]
    r_s, v_s, kk_s, bonus_s, g_s, lw_s, kh_s, bb_s, y_s = refs[pos + 3:]
    L = RWKV_CHUNK
    n_chunks = t_len // L
    lo = _low_half_mask((t_len, LANES))

    r = _centred_shift(zr_ref[...], mur_ref)
    k = _centred_shift(zk_ref[...], muk_ref)
    v = _centred_shift(zv_ref[...], muv_ref)
    s = _centred_shift(zs_ref[...], mus_ref)
    tw = jnp.tanh(s[:, 0:LANES])
    ad = s[:, LANES:2 * LANES]
    g_s[...] = _dot3(jax.nn.sigmoid(s[:, 2 * LANES:]), g2_ref[...])
    kk = k * kkg_ref[...]
    kk = kk / jnp.maximum(jnp.sqrt(_seg_sum(kk * kk, lo)), 1e-12)
    kh_sum = jnp.zeros_like(k)
    for d in range(2):
        keep = lo if d == 0 else jnp.logical_not(lo)
        w_pre = w0_ref[d:d + 1, :] + _dot3(jnp.where(keep, tw, 0.0), w2_ref[...])
        neg = -w_pre
        softplus = jnp.maximum(neg, 0.0) + jnp.log1p(jnp.exp(-jnp.abs(neg)))
        lw_s[d] = -jnp.exp(-softplus - 0.5)
        a = jax.nn.sigmoid(a0_ref[d:d + 1, :] + _dot3(jnp.where(keep, ad, 0.0), a2_ref[...]))
        kh = k * (1.0 + (a - 1.0) * ka_ref[...])
        kh_s[d] = kh
        bb_s[d] = kk * a
        kh_sum = kh_sum + kh
    bonus_s[...] = _seg_sum(r * kh_sum * rk_ref[...], lo) * v
    r_s[...] = r
    v_s[...] = v
    kk_s[...] = kk

    ri = lax.broadcasted_iota(I32, (2 * L, 2 * L), 0)
    ci = lax.broadcasted_iota(I32, (2 * L, 2 * L), 1)
    tt, ss = ri & (L - 1), ci & (L - 1)
    eye = ri == ci
    ti = lax.broadcasted_iota(I32, (L, L), 0)
    si = lax.broadcasted_iota(I32, (L, L), 1)
    lo_c = _low_half_mask((L, LANES))
    consts = ((jnp.where(si <= ti, 1.0, 0.0).astype(BF16), tt > ss, tt >= ss, eye, lo_c),
              (jnp.where(si >= ti, 1.0, 0.0).astype(BF16), tt < ss, tt <= ss, eye, lo_c))

    if has_s0:
        sf_ref[...] = s0f_ref[...]
        sb_ref[...] = s0b_ref[...]
    else:
        sf_ref[...] = jnp.zeros((2 * L, 2 * L), F32)
        sb_ref[...] = jnp.zeros((2 * L, 2 * L), F32)

    unroll = max(u for u in range(1, RWKV_UNROLL + 1) if n_chunks % u == 0)

    def group(i, carry):
        units = []
        for u in range(unroll):
            c = i * unroll + u
            starts = (c * L, (n_chunks - 1 - c) * L)
            if not isinstance(i, int):
                starts = tuple(pl.multiple_of(s, L) for s in starts)
            units += [(0, starts[0]), (1, starts[1])]
        local = _rwkv_chunks(units, r_s, v_s, kk_s, lw_s, kh_s, bb_s, consts)
        states = [sf_ref[...], sb_ref[...]]
        for (d, start), loc in zip(units, local):
            y_d, states[d] = _rwkv_apply(states[d], loc)
            y_s[d, pl.ds(start, L), :] = y_d
        sf_ref[...] = states[0]
        sb_ref[...] = states[1]
        return carry

    if n_chunks == unroll:
        group(0, 0)
    else:
        lax.fori_loop(0, n_chunks // unroll, group, 0)

    y = y_s[0] + y_s[1]
    inv_n = 1.0 / RWKV_HEAD
    mu = _seg_sum(y, lo) * inv_n
    yc = y - mu
    var = _seg_sum(yc * yc, lo) * inv_n
    yn = yc * lax.rsqrt(var + GN_EPS) * ln_ref[0:1, :] + ln_ref[1:2, :]
    o_ref[...] = ((yn + bonus_s[...]) * g_s[...]).astype(o_ref.dtype)


def _rwkv(z, out, par, nb, t_len, row0, s0=None):
    c = par['w0'].shape[1]
    pairs = c // LANES
    assert t_len % RWKV_CHUNK == 0 and row0 % t_len == 0 and (3 * c) % (4 * LANES) == 0
    rb0 = row0 // t_len
    cb = c // LANES
    sb = 3 * c // (4 * LANES)
    zmap = lambda off: (lambda b, j: (rb0 + b, off + j))
    cmap = lambda off: (lambda b, j: (0, off + j))
    in_specs = [pl.BlockSpec((t_len, LANES), zmap(0)), pl.BlockSpec((t_len, LANES), zmap(cb)),
                pl.BlockSpec((t_len, LANES), zmap(2 * cb)),
                pl.BlockSpec((t_len, 4 * LANES), lambda b, j: (rb0 + b, sb)),
                pl.BlockSpec((2, LANES), cmap(0)), pl.BlockSpec((2, LANES), cmap(cb)),
                pl.BlockSpec((2, LANES), cmap(2 * cb)), pl.BlockSpec((2, 4 * LANES), lambda b, j: (0, sb)),
                pl.BlockSpec((2, LANES), cmap(0)), pl.BlockSpec((LANES, LANES), cmap(0)),
                pl.BlockSpec((2, LANES), cmap(0)), pl.BlockSpec((LANES, LANES), cmap(0)),
                pl.BlockSpec((2 * LANES, LANES), cmap(0)),
                pl.BlockSpec((1, LANES), cmap(0)), pl.BlockSpec((1, LANES), cmap(0)),
                pl.BlockSpec((1, LANES), cmap(0)), pl.BlockSpec((2, LANES), cmap(0))]
    args = [z, z, z, z, par['mu'], par['mu'], par['mu'], par['mu'], par['w0'], par['w2'], par['a0'], par['a2'],
            par['g2'], par['k_k'], par['k_a'], par['r_k'], par['ln']]
    smap = lambda b, j: (b, j, 0, 0)
    if s0 is not None:
        in_specs += [pl.BlockSpec((None, None, LANES, LANES), smap)] * 2
        args += [s0[0], s0[1]]
    in_specs.append(pl.BlockSpec(memory_space=pl.ANY))
    args.append(out)
    st_shape = jax.ShapeDtypeStruct((nb, pairs, LANES, LANES), F32)
    tl = pltpu.VMEM((t_len, LANES), F32)
    tl2 = pltpu.VMEM((2, t_len, LANES), F32)

    def kern(*refs):
        n_in = len(args)
        _rwkv_kernel(*refs[:n_in - 1], *refs[n_in:], t_len=t_len, has_s0=s0 is not None)

    return pl.pallas_call(
        kern,
        grid=(nb, pairs),
        in_specs=in_specs,
        out_specs=[pl.BlockSpec((t_len, LANES), zmap(0)),
                   pl.BlockSpec((None, None, LANES, LANES), smap),
                   pl.BlockSpec((None, None, LANES, LANES), smap)],
        out_shape=[jax.ShapeDtypeStruct(out.shape, out.dtype), st_shape, st_shape],
        scratch_shapes=[tl, tl, tl, tl, tl, tl2, tl2, tl2, tl2],
        input_output_aliases={len(args) - 1: 0},
        compiler_params=_cparams(("arbitrary", "arbitrary")),
    )(*args)


def _rwkv_params(p):
    c = p['w0'].shape[1]
    return dict(mu=p['mu'], w0=p['w0'], w2=p['w2'].reshape(-1, c), a0=p['a0'], a2=p['a2'].reshape(-1, c),
                g2=p['g2'], k_k=p['k_k'].reshape(1, c), k_a=p['k_a'].reshape(1, c), r_k=p['r_k'].reshape(1, c),
                ln=p['ln'])


def _states_to_blockdiag(s):
    b, h, n, _ = s.shape
    st = jnp.swapaxes(s, 2, 3).reshape(b, h // 2, 2, n, n)
    z = jnp.zeros((b, h // 2, n, n), s.dtype)
    top = jnp.concatenate([st[:, :, 0], z], axis=-1)
    bot = jnp.concatenate([z, st[:, :, 1]], axis=-1)
    return jnp.concatenate([top, bot], axis=-2)


def _blockdiag_to_states(bd):
    b, hp, _, _ = bd.shape
    n = RWKV_HEAD
    st = jnp.stack([bd[:, :, :n, :n], bd[:, :, n:, n:]], axis=2)
    return jnp.swapaxes(st, 3, 4).reshape(b, 2 * hp, n, n)


def _ffn_pre_kernel(y_ref, g_ref, sh_ref, sc_ref, wh_ref, wl_ref, b_ref, hp_ref, route_ref, *, n_exp, n_grp):
    y = y_ref[...]
    ms = jnp.mean(y * y, axis=-1, keepdims=True)
    h = y * lax.rsqrt(ms + NORM_EPS) * g_ref[...]
    h = h * (1.0 + sc_ref[...]) + sh_ref[...]
    hp_ref[...] = h
    hh, hl = _split(h)
    wh = wh_ref[...]
    lg = _dot(hh, wh) + _dot(hl, wh) + _dot(hh, wl_ref[...]) + b_ref[...]
    lane = lax.broadcasted_iota(I32, lg.shape, 1)
    lane_f = lane.astype(F32)
    big = 1e9
    is_g = (lane >= n_exp) & (lane < n_exp + n_grp)
    gl = jnp.where(is_g, lg, NEG_BIG)
    gmax = jnp.max(gl, axis=-1, keepdims=True)
    gidx = jnp.min(jnp.where(is_g & (gl == gmax), lane_f, big), axis=-1, keepdims=True) - n_exp
    gprob = 1.0 / jnp.sum(jnp.where(is_g, jnp.exp(gl - gmax), 0.0), axis=-1, keepdims=True)
    gstart = gidx * MOE_GROUP_EXPERTS
    in_grp = (lane_f >= gstart) & (lane_f < gstart + MOE_GROUP_EXPERTS)
    el = jnp.where(in_grp, lg, NEG_BIG)
    e1v = jnp.max(el, axis=-1, keepdims=True)
    e1i = jnp.min(jnp.where(in_grp & (el == e1v), lane_f, big), axis=-1, keepdims=True)
    rest = in_grp & (lane_f != e1i)
    el2 = jnp.where(rest, lg, NEG_BIG)
    e2v = jnp.max(el2, axis=-1, keepdims=True)
    e2i = jnp.min(jnp.where(rest & (el2 == e2v), lane_f, big), axis=-1, keepdims=True)
    t = jnp.exp(e2v - e1v)
    den = 1.0 + t
    w1 = gprob / den
    w2 = gprob * t / den
    route_ref[...] = jnp.where(lane == 0, e1i, jnp.where(lane == 1, e2i,
                               jnp.where(lane == 2, w1, jnp.where(lane == 3, w2, 0.0))))


def _ffn_pre(y, gain, m6, rows, wr_hi, wr_lo, br, n_exp, n_grp):
    n, d = y.shape
    tm = _pick(np.gcd(rows.n_p, rows.ts), (256, 128, 64, 32, 16, 8))
    grp = rows.group_of_block(tm)
    const = lambda i: (0, 0)
    return pl.pallas_call(
        functools.partial(_ffn_pre_kernel, n_exp=n_exp, n_grp=n_grp),
        grid=(n // tm,),
        in_specs=[pl.BlockSpec((tm, d), lambda i: (i, 0)),
                  pl.BlockSpec((1, d), const),
                  pl.BlockSpec((None, 1, d), lambda i: (grp(i) * 6 + 3, 0, 0)),
                  pl.BlockSpec((None, 1, d), lambda i: (grp(i) * 6 + 4, 0, 0)),
                  pl.BlockSpec((d, LANES), const), pl.BlockSpec((d, LANES), const),
                  pl.BlockSpec((1, LANES), const)],
        out_specs=[pl.BlockSpec((tm, d), lambda i: (i, 0)), pl.BlockSpec((tm, LANES), lambda i: (i, 0))],
        out_shape=[jax.ShapeDtypeStruct((n, d), F32), jax.ShapeDtypeStruct((n, LANES), F32)],
        compiler_params=_cparams(("arbitrary",)),
    )(y, gain.reshape(1, d), m6, m6, wr_hi, wr_lo, br)


def _dispatch_kernel(dest_ref, hp_ref, _zeros_ref, xs_ref, sem, *, tm):
    base = pl.program_id(0) * tm

    def row_copy(src_row, dst_row):
        return pltpu.make_async_copy(hp_ref.at[pl.ds(src_row, 1), :], xs_ref.at[pl.ds(dst_row, 1), :], sem)

    def start(r, carry):
        a = MOE_TOPK * (base + r)
        for k in range(MOE_TOPK):
            row_copy(r, dest_ref[a + k]).start()
        return carry

    def wait(r, carry):
        for k in range(MOE_TOPK):
            row_copy(0, 0).wait()
        return carry

    lax.fori_loop(0, tm, start, 0)
    lax.fori_loop(0, tm, wait, 0)


def _moe_dispatch(dest, hp, n_slots):
    n = hp.shape[0]
    tm = _pick(n, (256, 128, 64, 32, 16, 8))
    any_spec = pl.BlockSpec(memory_space=pl.ANY)
    return pl.pallas_call(
        functools.partial(_dispatch_kernel, tm=tm),
        grid_spec=pltpu.PrefetchScalarGridSpec(
            num_scalar_prefetch=1, grid=(n // tm,),
            in_specs=[pl.BlockSpec((tm, hp.shape[1]), lambda i, dest: (i, 0)), any_spec], out_specs=any_spec,
            scratch_shapes=[pltpu.SemaphoreType.DMA]),
        out_shape=jax.ShapeDtypeStruct((n_slots, hp.shape[1]), hp.dtype),
        input_output_aliases={2: 0},
        compiler_params=_cparams(("arbitrary",)),
    )(dest, hp, jnp.zeros((n_slots, hp.shape[1]), hp.dtype))


def _expert_kernel(blk_e_ref, nact_ref, x_ref, wg_ref, wu_ref, wd_ref, o_ref, xb_ref):
    i = pl.program_id(0)
    j = pl.program_id(1)

    @pl.when(i < nact_ref[0])
    def _():
        @pl.when(j == 0)
        def _():
            xb_ref[...] = x_ref[...].astype(BF16)

        xb = xb_ref[...]
        gate = _dot(xb, wg_ref[...].astype(BF16))
        hid = (gate * jax.nn.sigmoid(gate) * _dot(xb, wu_ref[...].astype(BF16))).astype(BF16)

        d = o_ref.shape[1]
        cw = _pick(d, (1024, 512, 256, 128))
        for c in range(d // cw):
            cols = slice(c * cw, (c + 1) * cw)
            part = _dot(hid, wd_ref[:, cols].astype(BF16))

            @pl.when(j == 0)
            def _():
                o_ref[:, cols] = part

            @pl.when(j > 0)
            def _():
                o_ref[:, cols] += part


def _moe_experts(xs, blk_e, nact, w_gate, w_up, w_down, layer):
    n_slots, d = xs.shape
    _, n_exp, _, hid = w_gate.shape
    bm = MOE_BLOCK_ROWS
    hb = _pick(hid, (MOE_HIDDEN_BLOCK, 128))
    nh = hid // hb

    def active(i, j, blk_e, nact):
        ib = jnp.minimum(i, nact[0] - 1)
        jb = jnp.where(i < nact[0], j, nh - 1)
        return ib, jb

    def xmap(i, j, blk_e, nact):
        return active(i, j, blk_e, nact)[0], 0

    def gmap(i, j, blk_e, nact):
        ib, jb = active(i, j, blk_e, nact)
        return layer, blk_e[ib], 0, jb

    def dmap(i, j, blk_e, nact):
        ib, jb = active(i, j, blk_e, nact)
        return layer, blk_e[ib], jb, 0

    return pl.pallas_call(
        _expert_kernel,
        grid_spec=pltpu.PrefetchScalarGridSpec(
            num_scalar_prefetch=2, grid=(n_slots // bm, nh),
            in_specs=[pl.BlockSpec((bm, d), xmap),
                      pl.BlockSpec((None, None, d, hb), gmap), pl.BlockSpec((None, None, d, hb), gmap),
                      pl.BlockSpec((None, None, hb, d), dmap)],
            out_specs=pl.BlockSpec((bm, d), xmap),
            scratch_shapes=[pltpu.VMEM((bm, d), BF16)]),
        out_shape=jax.ShapeDtypeStruct((n_slots, d), F32),
        input_output_aliases={2: 0},
        compiler_params=_cparams(("arbitrary", "arbitrary")),
    )(blk_e, nact, xs, w_gate, w_up, w_down)


def _combine_kernel(dest_ref, y_ref, gate_ref, w_ref, ys_ref, o_ref, buf, sem, *, tm, n_steps):
    i = pl.program_id(0)
    slot = lax.rem(i, 2)

    def row_copy(src_row, s, k, r):
        return pltpu.make_async_copy(ys_ref.at[pl.ds(src_row, 1), :], buf.at[s, k, pl.ds(r, 1), :], sem.at[s])

    def issue(step, s):
        def body(r, carry):
            a = 2 * (step * tm + r)
            row_copy(dest_ref[a], s, 0, r).start()
            row_copy(dest_ref[a + 1], s, 1, r).start()
            return carry
        lax.fori_loop(0, tm, body, 0)

    @pl.when(i == 0)
    def _():
        issue(0, 0)

    @pl.when(i + 1 < n_steps)
    def _():
        issue(i + 1, 1 - slot)

    def wait(r, carry):
        row_copy(0, slot, 0, 0).wait()
        row_copy(0, slot, 1, 0).wait()
        return carry
    lax.fori_loop(0, tm, wait, 0)

    w0 = w_ref[:, 0:1]
    w1 = w_ref[:, 1:2]
    o_ref[...] = y_ref[...] + gate_ref[...] * (w0 * buf[slot, 0] + w1 * buf[slot, 1])


def _moe_combine(dest, y, m6, rows, wts, ys):
    n, d = y.shape
    tm = _pick(np.gcd(rows.n_p, rows.ts), (128, 64, 32, 16, 8))
    grp = rows.group_of_block(tm)
    n_steps = n // tm
    return pl.pallas_call(
        functools.partial(_combine_kernel, tm=tm, n_steps=n_steps),
        grid_spec=pltpu.PrefetchScalarGridSpec(
            num_scalar_prefetch=1, grid=(n_steps,),
            in_specs=[pl.BlockSpec((tm, d), lambda i, dest: (i, 0)),
                      pl.BlockSpec((None, 1, d), lambda i, dest: (grp(i) * 6 + 5, 0, 0)),
                      pl.BlockSpec((tm, MOE_TOPK), lambda i, dest: (i, 0)),
                      pl.BlockSpec(memory_space=pl.ANY)],
            out_specs=pl.BlockSpec((tm, d), lambda i, dest: (i, 0)),
            scratch_shapes=[pltpu.VMEM((2, MOE_TOPK, tm, d), F32), pltpu.SemaphoreType.DMA((2,))]),
        out_shape=jax.ShapeDtypeStruct((n, d), F32),
        compiler_params=_cparams(("arbitrary",)),
    )(dest, y, m6, wts, ys)


def _moe_layer(y, gain, m6, rows, w_group, b_group, w_expert, b_expert, w_gate, w_up, w_down, layer):
    n, d = y.shape
    n_exp = w_expert.shape[1]
    n_grp = w_group.shape[1]
    assert n_exp + n_grp <= LANES and n_exp == n_grp * MOE_GROUP_EXPERTS
    pad = jnp.zeros((d, LANES - n_exp - n_grp), F32)
    wr = jnp.concatenate([w_expert, w_group, pad], axis=1)
    wr_hi, wr_lo = _split(wr)
    br = jnp.concatenate([b_expert, b_group, jnp.zeros((LANES - n_exp - n_grp,), F32)]).reshape(1, LANES)
    hp, route = _ffn_pre(y, gain, m6, rows, wr_hi, wr_lo, br, n_exp, n_grp)

    bm = MOE_BLOCK_ROWS
    n_assign = n * MOE_TOPK
    flat_e = route[:, :MOE_TOPK].astype(I32).reshape(n_assign)
    wts = route[:, MOE_TOPK:2 * MOE_TOPK]
    onehot = (flat_e[:, None] == jnp.arange(n_exp, dtype=I32)[None, :]).astype(I32)
    csum = jnp.cumsum(onehot, axis=0)
    counts = csum[-1]
    padded = (counts + bm - 1) // bm * bm
    pad_end = jnp.cumsum(padded)
    dest = jnp.sum(onehot * (csum - 1 + (pad_end - padded)[None, :]), axis=1)
    n_blocks = (n_assign + n_exp * (bm - 1) + bm - 1) // bm
    blk_start = jnp.arange(n_blocks, dtype=I32) * bm
    blk_e = jnp.minimum(jnp.sum((pad_end[None, :] <= blk_start[:, None]).astype(I32), axis=1), n_exp - 1)
    nact = (pad_end[-1:] // bm).astype(I32)

    xs = _moe_dispatch(dest, hp, n_blocks * bm)
    ys = _moe_experts(xs, blk_e, nact, w_gate, w_up, w_down, layer)
    return _moe_combine(dest, y, m6, rows, wts, ys)


def _rope_rows(t_len, rot_dim):
    rows = t_len // GRID_W
    row = jnp.repeat(jnp.arange(rows, dtype=F32), GRID_W)
    col = jnp.tile(jnp.arange(GRID_W, dtype=F32), rows)
    quarter = rot_dim // 4
    inv = ROPE_THETA ** (-jnp.arange(quarter, dtype=F32) / quarter)
    ang = jnp.concatenate([row[:, None] * inv, col[:, None] * inv], axis=-1)
    cos, sin = jnp.cos(ang), jnp.sin(ang)
    reps = LANES // rot_dim
    return (jnp.tile(jnp.concatenate([cos, cos], axis=-1), (1, reps)),
            jnp.tile(jnp.concatenate([-sin, sin], axis=-1), (1, reps)))


def _token_tables(rows, rot_dim):
    cos_s, sin_s = _rope_rows(rows.ts, rot_dim)
    cos = jnp.concatenate([jnp.ones((rows.n_p, LANES), F32)] + [cos_s] * rows.bs, axis=0)
    sin = jnp.concatenate([jnp.zeros((rows.n_p, LANES), F32)] + [sin_s] * rows.bs, axis=0)
    return cos, sin


def _kv_order(rows, tok, ctx):
    parts = []
    for b in range(rows.bs):
        parts += [ctx[b], tok[rows.n_p + b * rows.ts:rows.n_p + (b + 1) * rows.ts]]
    parts.append(tok[:rows.n_p])
    return jnp.concatenate(parts, axis=0)


def _pad_rows8(rows_list):
    a = jnp.stack(rows_list, axis=0)
    return jnp.concatenate([a, jnp.zeros((8 - a.shape[0], a.shape[1]), a.dtype)], axis=0)


def _ab_layer(y, h, m6, rows, past, p, ctx):
    n, d = y.shape
    q_rank = p['q_norm'].shape[0]
    kv_rank = p['kv_norm'].shape[0]
    heads = p['w_uq'].shape[1] // (MLA_NOPE + MLA_ROPE)
    mla_in = q_rank + kv_rank + MLA_ROPE
    half = MLA_ROPE // 2
    swap = np.concatenate([np.arange(half, MLA_ROPE), np.arange(0, half)])

    w_in = p['w_in']
    w_kr = w_in[:, q_rank + kv_rank:mla_in]
    w_mla = jnp.concatenate([w_in[:, :q_rank + kv_rank], w_kr, w_kr, w_kr[:, swap], w_kr[:, swap]], axis=1).astype(BF16)
    zm = _matmul([h], [w_mla])
    z_rwkv = _matmul([h], [w_in[:, mla_in:].astype(BF16)])

    cqn, ckvn, ckvn_b = _mla_mid(zm, p['q_norm'], p['kv_norm'], q_rank, kv_rank)
    w_uq = p['w_uq'].reshape(q_rank, heads, MLA_NOPE + MLA_ROPE)
    w_rope = w_uq[:, :, MLA_NOPE:]
    w_q = jnp.concatenate([w_uq[:, :, :MLA_NOPE].reshape(q_rank, -1), w_rope.reshape(q_rank, -1),
                           w_rope[:, :, swap].reshape(q_rank, -1)], axis=1).astype(BF16)
    zq = _matmul([cqn], [w_q])
    cos_t, sin_t = _token_tables(rows, MLA_ROPE)

    def gains(g):
        g_r = g[MLA_NOPE:]
        return _pad_rows8([g[:MLA_NOPE], jnp.concatenate([g_r, g_r]), jnp.concatenate([g_r[swap], g_r[swap]])])

    scale = float(MLA_NOPE + MLA_ROPE) ** -0.5
    qn, qr = _mla_q_prep(zq, gains(p['qk_norm'][0]), cos_t, sin_t, heads, scale)

    ctx_ckv, ctx_kr, s_f0, s_b0 = ctx
    w_ukv = p['w_ukv'].reshape(kv_rank, heads, MLA_NOPE + MLA_V)
    w_kv = jnp.concatenate([w_ukv[:, :, :MLA_NOPE].reshape(kv_rank, -1), w_ukv[:, :, MLA_NOPE:].reshape(kv_rank, -1)],
                           axis=1).astype(BF16)
    zkv = _matmul([_kv_order(rows, ckvn_b, ctx_ckv.astype(BF16))], [w_kv])
    kr_off = q_rank + kv_rank
    ctx_kr2 = jnp.concatenate([ctx_kr, ctx_kr], axis=-1)
    kr2 = _kv_order(rows, zm[:, kr_off:kr_off + LANES], ctx_kr2)
    krs2 = _kv_order(rows, zm[:, kr_off + LANES:kr_off + 2 * LANES], ctx_kr2)
    one_c = jnp.ones((rows.bs, past, LANES), F32)
    cos_kv = _kv_order(rows, cos_t, one_c)
    sin_kv = _kv_order(rows, sin_t, 0.0 * one_c)
    kn, krh, vv = _mla_k_prep(zkv, kr2, krs2, gains(p['qk_norm'][1]), cos_kv, sin_kv, heads)

    o_mla = jnp.zeros((n, heads * MLA_V), BF16)
    o_mla = _mla_attn(qn, qr, kn, krh, vv, o_mla, heads, rows.bp, rows.tp, rows.tp, 0, rows.bs * (past + rows.ts))
    o_mla = _mla_attn(qn, qr, kn, krh, vv, o_mla, heads, rows.bs, rows.ts, past + rows.ts, rows.n_p, 0)

    par = _rwkv_params(p)
    c = par['w0'].shape[1]
    o_rwkv = jnp.zeros((n, c), BF16)
    o_rwkv, sf, sb = _rwkv(z_rwkv, o_rwkv, par, rows.bp, rows.tp, 0)
    s0 = (_states_to_blockdiag(s_f0), _states_to_blockdiag(s_b0))
    o_rwkv, _, _ = _rwkv(z_rwkv, o_rwkv, par, rows.bs, rows.ts, rows.n_p, s0)

    hm = heads * MLA_V
    w_out = p['w_out'].astype(BF16)
    y = _matmul([o_mla, o_rwkv], [w_out[:hm], w_out[hm:]], res=y, m6=m6, gate_idx=2, rows=rows)
    new_ckv = ckvn[:rows.n_p].reshape(rows.bp, rows.tp, kv_rank)
    new_kr = zm[:rows.n_p, kr_off:kr_off + MLA_ROPE].reshape(rows.bp, rows.tp, MLA_ROPE)
    return y, (new_ckv, new_kr, _blockdiag_to_states(sf), _blockdiag_to_states(sb))


def _gqa_layer(y, h, m6, rows, past, p, ctx):
    n, d = y.shape
    heads = p['w_out'].shape[0] // ATTN_HEAD
    kv_heads = heads // ATTN_GROUP
    nq, nk = heads * ATTN_HEAD, kv_heads * ATTN_HEAD
    z = _matmul([h], [p['w_in'].astype(BF16)])
    cos_t, sin_t = _token_tables(rows, ATTN_HEAD)
    g2 = _pad_rows8([p['qk_norm'][0], p['qk_norm'][1]])
    q, k_keep, k_rot, v_b = _gqa_prep(z, g2, cos_t, sin_t, heads, kv_heads, float(ATTN_HEAD) ** -0.5)
    ctx_k, ctx_v = ctx
    k_all = _kv_order(rows, k_rot, ctx_k.reshape(rows.bs, past, nk).astype(BF16))
    v_all = _kv_order(rows, v_b, ctx_v.reshape(rows.bs, past, nk).astype(BF16))
    o = jnp.zeros((n, nq), BF16)
    o = _gqa_attn(q, k_all, v_all, o, kv_heads, rows.bp, rows.tp, rows.tp, 0, rows.bs * (past + rows.ts))
    o = _gqa_attn(q, k_all, v_all, o, kv_heads, rows.bs, rows.ts, past + rows.ts, rows.n_p, 0)
    y = _matmul([o], [p['w_out'].astype(BF16)], res=y, m6=m6, gate_idx=2, rows=rows)
    new_k = k_keep[:rows.n_p].reshape(rows.bp, rows.tp, kv_heads, ATTN_HEAD)
    new_v = z[:rows.n_p, nq + nk:].reshape(rows.bp, rows.tp, kv_heads, ATTN_HEAD)
    return y, (new_k, new_v)


def kernel(x_prompt, x_sample, c, c_ctx, cache_mla_ckv, cache_mla_krope, state_rwkv_fwd, state_rwkv_bwd, cache_attn_k, cache_attn_v, mod_w, mod_b, norm_mix, norm_ffn, ab_w_in, ab_w_out, mla_q_norm, mla_w_uq, mla_kv_norm, mla_w_ukv, mla_qk_norm, rwkv_mu, rwkv_w0, rwkv_w2, rwkv_a0, rwkv_a2, rwkv_g2, rwkv_k_k, rwkv_k_a, rwkv_r_k, rwkv_ln, gqa_w_in, gqa_qk_norm, gqa_w_out, moe_w_group, moe_b_group, moe_w_expert, moe_b_expert, moe_w_gate, moe_w_up, moe_w_down):
    bp, tp, d = x_prompt.shape
    bs, ts, _ = x_sample.shape
    past = cache_mla_ckv.shape[2]
    depth = mod_w.shape[0]
    rows = _Rows(bp, tp, bs, ts)
    assert bs + 1 <= 8

    cond8 = jnp.concatenate([c_ctx[None, :], c, jnp.zeros((8 - 1 - bs, d), F32)], axis=0)
    m_all = _adaln_all(cond8, mod_w, mod_b)
    y = jnp.concatenate([x_prompt.reshape(bp * tp, d), x_sample.reshape(bs * ts, d)], axis=0)

    ab_out, gqa_out = [], []
    for layer in range(depth):
        m6 = m_all[layer].reshape(8 * 6, 1, d)
        h = _norm_mod(y, norm_mix[layer], m6, rows, 0, 1)
        i = layer // 2
        if layer % 2 == 0:
            p = dict(w_in=ab_w_in[i], w_out=ab_w_out[i], q_norm=mla_q_norm[i], w_uq=mla_w_uq[i],
                     kv_norm=mla_kv_norm[i], w_ukv=mla_w_ukv[i], qk_norm=mla_qk_norm[i], mu=rwkv_mu[i],
                     w0=rwkv_w0[i], w2=rwkv_w2[i], a0=rwkv_a0[i], a2=rwkv_a2[i], g2=rwkv_g2[i],
                     k_k=rwkv_k_k[i], k_a=rwkv_k_a[i], r_k=rwkv_r_k[i], ln=rwkv_ln[i])
            y, new = _ab_layer(y, h, m6, rows, past, p, (cache_mla_ckv[:, i], cache_mla_krope[:, i],
                                                          state_rwkv_fwd[:, i], state_rwkv_bwd[:, i]))
            ab_out.append(new)
        else:
            p = dict(w_in=gqa_w_in[i], qk_norm=gqa_qk_norm[i], w_out=gqa_w_out[i])
            y, new = _gqa_layer(y, h, m6, rows, past, p, (cache_attn_k[:, i], cache_attn_v[:, i]))
            gqa_out.append(new)
        y = _moe_layer(y, norm_ffn[layer], m6, rows, moe_w_group[layer], moe_b_group[layer], moe_w_expert[layer],
                       moe_b_expert[layer], moe_w_gate, moe_w_up, moe_w_down, layer)

    yp = y[:rows.n_p].reshape(bp, tp, d)
    ys = y[rows.n_p:].reshape(bs, ts, d)
    stack = lambda outs, k: jnp.stack([o[k] for o in outs], axis=1)
    return (yp, ys, stack(ab_out, 0), stack(ab_out, 1), stack(ab_out, 2), stack(ab_out, 3),
            stack(gqa_out, 0), stack(gqa_out, 1))
```

```python
import functools

import numpy as np
import jax
import jax.numpy as jnp
from jax import lax
from jax.experimental import pallas as pl
from jax.experimental.pallas import tpu as pltpu

F32 = jnp.float32
BF16 = jnp.bfloat16
I32 = jnp.int32

GRID_W = 64
ROPE_THETA = 10000.0
NORM_EPS = 1e-6
GN_EPS = 64e-5
MLA_NOPE = 128
MLA_ROPE = 64
MLA_V = 128
RWKV_HEAD = 64
ATTN_HEAD = 128
ATTN_GROUP = 4
MOE_TOPK = 2
MOE_GROUP_EXPERTS = 8

LANES = 128
RWKV_CHUNK = 64
RWKV_UNROLL = 16
MOE_BLOCK_ROWS = 384
MOE_HIDDEN_BLOCK = 256
DMA_LOOP_UNROLL = 8
ATTN_KV_BLOCK_BYTES = 4 * 1024 * 1024
VMEM_LIMIT = 56 * 1024 * 1024
NEG_BIG = -1e30


def _pick(n, prefs):
    for p in prefs:
        if n % p == 0:
            return p
    return n


def _cparams(sem):
    return pltpu.CompilerParams(dimension_semantics=sem, vmem_limit_bytes=VMEM_LIMIT)


def _dot(a, b):
    return jnp.dot(a, b, preferred_element_type=F32)


def _dot_nt(a, b):
    return lax.dot_general(a, b, (((1,), (1,)), ((), ())), preferred_element_type=F32)


def _dot_tn(a, b):
    return lax.dot_general(a, b, (((0,), (0,)), ((), ())), preferred_element_type=F32)


def _split(x):
    hi = x.astype(BF16)
    lo = (x - hi.astype(F32)).astype(BF16)
    return hi, lo


def _dot3(a, b):
    ah, al = _split(a)
    bh, bl = _split(b)
    return _dot(ah, bh) + _dot(ah, bl) + _dot(al, bh)


def _mod_kernel(c_ref, w_ref, b_ref, o_ref):
    c = c_ref[...]
    s = c * jax.nn.sigmoid(c)
    sh, sl = _split(s)
    w = w_ref[...].astype(BF16)
    o_ref[...] = _dot(sh, w) + _dot(sl, w) + b_ref[...]


def _adaln_all(cond8, mod_w, mod_b):
    depth, d, n6 = mod_w.shape
    tn = _pick(n6, (512, 256, 128))
    return pl.pallas_call(
        _mod_kernel,
        grid=(depth, n6 // tn),
        in_specs=[pl.BlockSpec((8, d), lambda l, j: (0, 0)),
                  pl.BlockSpec((None, d, tn), lambda l, j: (l, 0, j)),
                  pl.BlockSpec((None, 1, tn), lambda l, j: (l, 0, j))],
        out_specs=pl.BlockSpec((None, 8, tn), lambda l, j: (l, 0, j)),
        out_shape=jax.ShapeDtypeStruct((depth, 8, n6), F32),
        compiler_params=_cparams(("arbitrary", "arbitrary")),
    )(cond8, mod_w, mod_b.reshape(depth, 1, n6))


class _Rows:
    def __init__(self, bp, tp, bs, ts):
        self.bp, self.tp, self.bs, self.ts = bp, tp, bs, ts
        self.n_p = bp * tp
        self.n = bp * tp + bs * ts

    def group_of_block(self, tm):
        n_p, ts = self.n_p, self.ts
        assert n_p % tm == 0 and ts % tm == 0
        return lambda i: jnp.where(i * tm < n_p, 0, 1 + (i * tm - n_p) // ts)


def _norm_mod_kernel(y_ref, g_ref, sh_ref, sc_ref, o_ref):
    y = y_ref[...]
    ms = jnp.mean(y * y, axis=-1, keepdims=True)
    h = y * lax.rsqrt(ms + NORM_EPS) * g_ref[...]
    h = h * (1.0 + sc_ref[...]) + sh_ref[...]
    o_ref[...] = h.astype(o_ref.dtype)


def _norm_mod(y, gain, m6, rows, shift_idx, scale_idx):
    n, d = y.shape
    tm = _pick(rows.tp, (256, 128, 64, 32, 16, 8))
    tm = tm if rows.ts % tm == 0 else _pick(np.gcd(rows.tp, rows.ts), (256, 128, 64, 32, 16, 8))
    grp = rows.group_of_block(tm)
    return pl.pallas_call(
        _norm_mod_kernel,
        grid=(n // tm,),
        in_specs=[pl.BlockSpec((tm, d), lambda i: (i, 0)),
                  pl.BlockSpec((1, d), lambda i: (0, 0)),
                  pl.BlockSpec((None, 1, d), lambda i: (grp(i) * 6 + shift_idx, 0, 0)),
                  pl.BlockSpec((None, 1, d), lambda i: (grp(i) * 6 + scale_idx, 0, 0))],
        out_specs=pl.BlockSpec((tm, d), lambda i: (i, 0)),
        out_shape=jax.ShapeDtypeStruct((n, d), BF16),
        compiler_params=_cparams(("arbitrary",)),
    )(y, gain.reshape(1, d), m6, m6)


def _mm_kernel(*refs, n_in, has_res):
    o_ref = refs[-1]
    acc = None
    for i in range(n_in):
        w = refs[n_in + i][...]
        if w.dtype != BF16:
            w = w.astype(BF16)
        part = _dot(refs[i][...], w)
        acc = part if acc is None else acc + part
    if has_res:
        acc = refs[2 * n_in][...] + refs[2 * n_in + 1][...] * acc
    o_ref[...] = acc.astype(o_ref.dtype)


def _matmul(xs, ws, out_dtype=F32, res=None, m6=None, gate_idx=None, rows=None, tm_prefs=(1024, 512, 256, 128, 64, 32, 16, 8)):
    m = xs[0].shape[0]
    n = ws[0].shape[1]
    if res is not None:
        cands = [t for t in tm_prefs if rows.n_p % t == 0 and rows.ts % t == 0]
        tm = cands[0]
    else:
        tm = _pick(m, tm_prefs)
    tn = _pick(n, (512, 384, 256, 128))
    in_specs = [pl.BlockSpec((tm, x.shape[1]), lambda i, j: (i, 0)) for x in xs]
    in_specs += [pl.BlockSpec((w.shape[0], tn), lambda i, j: (0, j)) for w in ws]
    args = list(xs) + list(ws)
    if res is not None:
        grp = rows.group_of_block(tm)
        in_specs += [pl.BlockSpec((tm, tn), lambda i, j: (i, j)),
                     pl.BlockSpec((None, 1, tn), lambda i, j: (grp(i) * 6 + gate_idx, 0, j))]
        args += [res, m6]
    return pl.pallas_call(
        functools.partial(_mm_kernel, n_in=len(xs), has_res=res is not None),
        grid=(m // tm, n // tn),
        in_specs=in_specs,
        out_specs=pl.BlockSpec((tm, tn), lambda i, j: (i, j)),
        out_shape=jax.ShapeDtypeStruct((m, n), out_dtype),
        compiler_params=_cparams(("arbitrary", "arbitrary")),
    )(*args)


def _mla_mid_kernel(z_ref, gq_ref, gkv_ref, cq_ref, ckv_ref, ckvb_ref, *, q_rank, kv_rank):
    cq = z_ref[:, :q_rank]
    ms = jnp.mean(cq * cq, axis=-1, keepdims=True)
    cq_ref[...] = (cq * lax.rsqrt(ms + NORM_EPS) * gq_ref[...]).astype(BF16)
    ckv = z_ref[:, q_rank:q_rank + kv_rank]
    ms = jnp.mean(ckv * ckv, axis=-1, keepdims=True)
    ckvn = ckv * lax.rsqrt(ms + NORM_EPS) * gkv_ref[...]
    ckv_ref[...] = ckvn
    ckvb_ref[...] = ckvn.astype(BF16)


def _mla_mid(zm, q_norm, kv_norm, q_rank, kv_rank):
    n, w = zm.shape
    tm = _pick(n, (512, 256, 128, 64, 32, 16, 8))
    return pl.pallas_call(
        functools.partial(_mla_mid_kernel, q_rank=q_rank, kv_rank=kv_rank),
        grid=(n // tm,),
        in_specs=[pl.BlockSpec((tm, w), lambda i: (i, 0)),
                  pl.BlockSpec((1, q_rank), lambda i: (0, 0)),
                  pl.BlockSpec((1, kv_rank), lambda i: (0, 0))],
        out_specs=[pl.BlockSpec((tm, q_rank), lambda i: (i, 0)),
                   pl.BlockSpec((tm, kv_rank), lambda i: (i, 0)),
                   pl.BlockSpec((tm, kv_rank), lambda i: (i, 0))],
        out_shape=[jax.ShapeDtypeStruct((n, q_rank), BF16),
                   jax.ShapeDtypeStruct((n, kv_rank), F32),
                   jax.ShapeDtypeStruct((n, kv_rank), BF16)],
        compiler_params=_cparams(("arbitrary",)),
    )(zm, q_norm.reshape(1, q_rank), kv_norm.reshape(1, kv_rank))


def _low_half_mask(shape):
    return lax.broadcasted_iota(I32, shape, 1) < (LANES // 2)


def _mla_q_prep_kernel(z_ref, g_ref, cos_ref, sin_ref, qn_ref, qr_ref, *, heads, scale):
    hn = heads * MLA_NOPE
    hr = heads * MLA_ROPE
    qk = float(MLA_NOPE + MLA_ROPE)
    lo = _low_half_mask(cos_ref.shape)
    cos = cos_ref[...]
    sin = sin_ref[...]
    g_n, g_r, g_s = g_ref[0:1, :], g_ref[1:2, :], g_ref[2:3, :]
    for p in range(heads // 2):
        rope = z_ref[:, hn + LANES * p:hn + LANES * (p + 1)]
        rsw = z_ref[:, hn + hr + LANES * p:hn + hr + LANES * (p + 1)]
        r2 = rope * rope
        ss_r = (jnp.sum(jnp.where(lo, r2, 0.0), axis=-1, keepdims=True),
                jnp.sum(jnp.where(lo, 0.0, r2), axis=-1, keepdims=True))
        inv = []
        for hh in range(2):
            h = 2 * p + hh
            nope = z_ref[:, MLA_NOPE * h:MLA_NOPE * (h + 1)]
            ss = jnp.sum(nope * nope, axis=-1, keepdims=True) + ss_r[hh]
            inv_h = lax.rsqrt(ss / qk + NORM_EPS) * scale
            inv.append(inv_h)
            qn_ref[:, MLA_NOPE * h:MLA_NOPE * (h + 1)] = (nope * inv_h * g_n).astype(BF16)
        inv2 = jnp.where(lo, inv[0], inv[1])
        qr_ref[:, LANES * p:LANES * (p + 1)] = ((rope * g_r * cos + rsw * g_s * sin) * inv2).astype(BF16)


def _mla_q_prep(zq, g3, cos, sin, heads, scale):
    n, w = zq.shape
    tm = _pick(n, (256, 128, 64, 32, 16, 8))
    return pl.pallas_call(
        functools.partial(_mla_q_prep_kernel, heads=heads, scale=scale),
        grid=(n // tm,),
        in_specs=[pl.BlockSpec((tm, w), lambda i: (i, 0)),
                  pl.BlockSpec((8, LANES), lambda i: (0, 0)),
                  pl.BlockSpec((tm, LANES), lambda i: (i, 0)),
                  pl.BlockSpec((tm, LANES), lambda i: (i, 0))],
        out_specs=[pl.BlockSpec((tm, heads * MLA_NOPE), lambda i: (i, 0)),
                   pl.BlockSpec((tm, heads * MLA_ROPE), lambda i: (i, 0))],
        out_shape=[jax.ShapeDtypeStruct((n, heads * MLA_NOPE), BF16),
                   jax.ShapeDtypeStruct((n, heads * MLA_ROPE), BF16)],
        compiler_params=_cparams(("arbitrary",)),
    )(zq, g3, cos, sin)


def _mla_k_prep_kernel(z_ref, kr_ref, krs_ref, g_ref, cos_ref, sin_ref, kn_ref, krh_ref, v_ref, *, heads):
    hn = heads * MLA_NOPE
    qk = float(MLA_NOPE + MLA_ROPE)
    lo = _low_half_mask(cos_ref.shape)
    g_n, g_r, g_s = g_ref[0:1, :], g_ref[1:2, :], g_ref[2:3, :]
    kr = kr_ref[...]
    ss_r = jnp.sum(jnp.where(lo, kr * kr, 0.0), axis=-1, keepdims=True)
    base = kr * g_r * cos_ref[...] + krs_ref[...] * g_s * sin_ref[...]
    for p in range(heads // 2):
        inv = []
        for hh in range(2):
            h = 2 * p + hh
            nope = z_ref[:, MLA_NOPE * h:MLA_NOPE * (h + 1)]
            ss = jnp.sum(nope * nope, axis=-1, keepdims=True) + ss_r
            inv_h = lax.rsqrt(ss / qk + NORM_EPS)
            inv.append(inv_h)
            kn_ref[:, MLA_NOPE * h:MLA_NOPE * (h + 1)] = (nope * inv_h * g_n).astype(BF16)
        krh_ref[:, LANES * p:LANES * (p + 1)] = (base * jnp.where(lo, inv[0], inv[1])).astype(BF16)
    v_ref[...] = z_ref[:, hn:].astype(BF16)


def _mla_k_prep(zkv, kr2, krs2, g3, cos, sin, heads):
    n, w = zkv.shape
    tm = _pick(n, (256, 128, 64, 32, 16, 8))
    hn = heads * MLA_NOPE
    row = lambda i: (i, 0)
    return pl.pallas_call(
        functools.partial(_mla_k_prep_kernel, heads=heads),
        grid=(n // tm,),
        in_specs=[pl.BlockSpec((tm, w), row), pl.BlockSpec((tm, LANES), row), pl.BlockSpec((tm, LANES), row),
                  pl.BlockSpec((8, LANES), lambda i: (0, 0)),
                  pl.BlockSpec((tm, LANES), row), pl.BlockSpec((tm, LANES), row)],
        out_specs=[pl.BlockSpec((tm, hn), row), pl.BlockSpec((tm, heads * MLA_ROPE), row),
                   pl.BlockSpec((tm, hn), row)],
        out_shape=[jax.ShapeDtypeStruct((n, hn), BF16), jax.ShapeDtypeStruct((n, heads * MLA_ROPE), BF16),
                   jax.ShapeDtypeStruct((n, hn), BF16)],
        compiler_params=_cparams(("arbitrary",)),
    )(zkv, kr2, krs2, g3, cos, sin)


def _softmax_pv(s, v):
    m = jnp.max(s, axis=-1, keepdims=True)
    p = jnp.exp(s - m)
    l = jnp.sum(p, axis=-1, keepdims=True)
    return _dot(p.astype(BF16), v) / l


def _mla_attn_kernel(qn_ref, qr_ref, kn_ref, kr_ref, v_ref, _aliased_out, o_ref, *, pairs):
    lo = _low_half_mask((qr_ref.shape[0], LANES))
    for p in range(pairs):
        qr = qr_ref[:, LANES * p:LANES * (p + 1)]
        kr = kr_ref[:, LANES * p:LANES * (p + 1)]
        zero = jnp.zeros_like(qr)
        for hh in range(2):
            h = 2 * p + hh
            sl = slice(MLA_NOPE * h, MLA_NOPE * (h + 1))
            qr_h = jnp.where(lo, qr, zero) if hh == 0 else jnp.where(lo, zero, qr)
            s = _dot_nt(qn_ref[:, sl], kn_ref[:, sl]) + _dot_nt(qr_h, kr)
            o_ref[:, sl] = _softmax_pv(s, v_ref[:, sl]).astype(o_ref.dtype)


def _mla_attn(qn, qr, kn, krh, v, out, heads, nb, tq_len, tk_len, q_row0, kv_row0):
    tq = _pick(tq_len, (256, 128, 64, 32, 16, 8))
    nq = tq_len // tq
    assert q_row0 % tq == 0 and kv_row0 % tk_len == 0
    qb0, kb0 = q_row0 // tq, kv_row0 // tk_len
    pairs = max(g for g in range(1, heads // 2 + 1)
                if (heads // 2) % g == 0 and (g == 1 or tk_len * g * 5 * LANES * 2 <= ATTN_KV_BLOCK_BYTES))
    qmap = lambda b, p, i: (qb0 + b * nq + i, p)
    kmap = lambda b, p, i: (kb0 + b, p)
    return pl.pallas_call(
        functools.partial(_mla_attn_kernel, pairs=pairs),
        grid=(nb, heads // 2 // pairs, nq),
        in_specs=[pl.BlockSpec((tq, 2 * MLA_NOPE * pairs), qmap), pl.BlockSpec((tq, LANES * pairs), qmap),
                  pl.BlockSpec((tk_len, 2 * MLA_NOPE * pairs), kmap), pl.BlockSpec((tk_len, LANES * pairs), kmap),
                  pl.BlockSpec((tk_len, 2 * MLA_V * pairs), kmap),
                  pl.BlockSpec(memory_space=pl.ANY)],
        out_specs=pl.BlockSpec((tq, 2 * MLA_V * pairs), qmap),
        out_shape=jax.ShapeDtypeStruct(out.shape, out.dtype),
        input_output_aliases={5: 0},
        compiler_params=_cparams(("arbitrary", "arbitrary", "arbitrary")),
    )(qn, qr, kn, krh, v, out)


def _gqa_prep_kernel(z_ref, g_ref, cos_ref, sin_ref, q_ref, kk_ref, kr_ref, v_ref, *, heads, kv_heads, scale):
    cos = cos_ref[...]
    sin = sin_ref[...]
    g_q, g_k = g_ref[0:1, :], g_ref[1:2, :]

    def normed(x, g):
        ms = jnp.mean(x * x, axis=-1, keepdims=True)
        return x * lax.rsqrt(ms + NORM_EPS) * g

    def rope(x):
        return x * cos + pltpu.roll(x, ATTN_HEAD // 2, 1) * sin

    for h in range(heads):
        sl = slice(ATTN_HEAD * h, ATTN_HEAD * (h + 1))
        q_ref[:, sl] = (rope(normed(z_ref[:, sl], g_q)) * scale).astype(BF16)
    for h in range(kv_heads):
        sl = slice(ATTN_HEAD * h, ATTN_HEAD * (h + 1))
        kn = normed(z_ref[:, ATTN_HEAD * (heads + h):ATTN_HEAD * (heads + h + 1)], g_k)
        kk_ref[:, sl] = kn
        kr_ref[:, sl] = rope(kn).astype(BF16)
    v_ref[...] = z_ref[:, ATTN_HEAD * (heads + kv_heads):].astype(BF16)


def _gqa_prep(z, g2, cos, sin, heads, kv_heads, scale):
    n, w = z.shape
    tm = _pick(n, (256, 128, 64, 32, 16, 8))
    row = lambda i: (i, 0)
    nq, nk = heads * ATTN_HEAD, kv_heads * ATTN_HEAD
    return pl.pallas_call(
        functools.partial(_gqa_prep_kernel, heads=heads, kv_heads=kv_heads, scale=scale),
        grid=(n // tm,),
        in_specs=[pl.BlockSpec((tm, w), row), pl.BlockSpec((8, LANES), lambda i: (0, 0)),
                  pl.BlockSpec((tm, LANES), row), pl.BlockSpec((tm, LANES), row)],
        out_specs=[pl.BlockSpec((tm, nq), row), pl.BlockSpec((tm, nk), row),
                   pl.BlockSpec((tm, nk), row), pl.BlockSpec((tm, nk), row)],
        out_shape=[jax.ShapeDtypeStruct((n, nq), BF16), jax.ShapeDtypeStruct((n, nk), F32),
                   jax.ShapeDtypeStruct((n, nk), BF16), jax.ShapeDtypeStruct((n, nk), BF16)],
        compiler_params=_cparams(("arbitrary",)),
    )(z, g2, cos, sin)


def _gqa_attn_kernel(q_ref, k_ref, v_ref, _aliased_out, o_ref, *, kv_per_step):
    for j in range(kv_per_step):
        k = k_ref[:, ATTN_HEAD * j:ATTN_HEAD * (j + 1)]
        v = v_ref[:, ATTN_HEAD * j:ATTN_HEAD * (j + 1)]
        for g in range(ATTN_GROUP):
            h = j * ATTN_GROUP + g
            sl = slice(ATTN_HEAD * h, ATTN_HEAD * (h + 1))
            o_ref[:, sl] = _softmax_pv(_dot_nt(q_ref[:, sl], k), v).astype(o_ref.dtype)


def _gqa_attn(q, k, v, out, kv_heads, nb, tq_len, tk_len, q_row0, kv_row0):
    tq = _pick(tq_len, (256, 128, 64, 32, 16, 8))
    nq = tq_len // tq
    assert q_row0 % tq == 0 and kv_row0 % tk_len == 0
    qb0, kb0 = q_row0 // tq, kv_row0 // tk_len
    per = max(g for g in range(1, kv_heads + 1)
              if kv_heads % g == 0 and (g == 1 or tk_len * g * 2 * ATTN_HEAD * 2 <= ATTN_KV_BLOCK_BYTES))
    qmap = lambda b, h, i: (qb0 + b * nq + i, h)
    kmap = lambda b, h, i: (kb0 + b, h)
    gw = ATTN_GROUP * ATTN_HEAD * per
    return pl.pallas_call(
        functools.partial(_gqa_attn_kernel, kv_per_step=per),
        grid=(nb, kv_heads // per, nq),
        in_specs=[pl.BlockSpec((tq, gw), qmap), pl.BlockSpec((tk_len, ATTN_HEAD * per), kmap),
                  pl.BlockSpec((tk_len, ATTN_HEAD * per), kmap), pl.BlockSpec(memory_space=pl.ANY)],
        out_specs=pl.BlockSpec((tq, gw), qmap),
        out_shape=jax.ShapeDtypeStruct(out.shape, out.dtype),
        input_output_aliases={3: 0},
        compiler_params=_cparams(("arbitrary", "arbitrary", "arbitrary")),
    )(q, k, v, out)


def _seg_sum(x, lo):
    s0 = jnp.sum(jnp.where(lo, x, 0.0), axis=-1, keepdims=True)
    s1 = jnp.sum(jnp.where(lo, 0.0, x), axis=-1, keepdims=True)
    return jnp.where(lo, s0, s1)


def _centred_shift(z, mu_ref):
    t = z.shape[0]
    ti = lax.broadcasted_iota(I32, z.shape, 0)
    prev = jnp.where(ti == 0, 0.0, pltpu.roll(z, 1, 0))
    nxt = jnp.where(ti == t - 1, 0.0, pltpu.roll(z, t - 1, 0))
    return z + mu_ref[0:1, :] * (prev - z) + mu_ref[1:2, :] * (nxt - z)


def _stack_heads(x, lo):
    return jnp.concatenate([jnp.where(lo, x, 0.0), jnp.where(lo, 0.0, x)], axis=0)


def _rwkv_chunks(units, r_s, v_s, kk_s, lw_s, kh_s, bb_s, consts):
    L = RWKV_CHUNK
    U = range(len(units))
    cst = [consts[d] for d, _ in units]
    rows = [pl.ds(c0, L) for _, c0 in units]
    lo, eye = cst[0][4], cst[0][3]
    lw = [lw_s[units[u][0], rows[u], :] for u in U]
    hi = [x.astype(BF16) for x in lw]
    r1 = [lw[u] - hi[u].astype(F32) for u in U]
    mid = [x.astype(BF16) for x in r1]
    low = [(r1[u] - mid[u].astype(F32)).astype(BF16) for u in U]
    cum = [_dot(cst[u][0], hi[u]) + _dot(cst[u][0], mid[u]) + _dot(cst[u][0], low[u]) for u in U]
    g_inc = [jnp.exp(x) for x in cum]
    g_inv = [jnp.exp(-x) for x in cum]
    kap = [_stack_heads(kk_s[rows[u], :] * jnp.exp(cum[u] - lw[u]), lo).astype(BF16) for u in U]
    g_end = [g_inc[u][L - 1:L, :] if units[u][0] == 0 else g_inc[u][0:1, :] for u in U]
    bt32 = [_stack_heads(bb_s[units[u][0], rows[u], :] * g_inv[u], lo) for u in U]
    kt32 = [_stack_heads(kh_s[units[u][0], rows[u], :] * g_inv[u], lo) for u in U]
    rt32 = [_stack_heads(r_s[rows[u], :] * g_inc[u], lo) for u in U]
    bt = [x.astype(BF16) for x in bt32]
    kt = [x.astype(BF16) for x in kt32]
    rt = [x.astype(BF16) for x in rt32]
    vr = [_stack_heads(v_s[rows[u], :], lo).astype(BF16) for u in U]
    n2 = 2 * L
    gram = [_dot_nt(jnp.concatenate([kap[u], rt[u]], axis=0), jnp.concatenate([bt[u], kt[u]], axis=0)) for u in U]
    lb = [jnp.where(cst[u][1], gram[u][:n2, :n2], 0.0) for u in U]
    lk = [jnp.where(cst[u][1], gram[u][:n2, n2:], 0.0).astype(BF16) for u in U]
    hb = [jnp.where(cst[u][2], gram[u][n2:, :n2], 0.0).astype(BF16) for u in U]
    hk = [jnp.where(cst[u][2], gram[u][n2:, n2:], 0.0).astype(BF16) for u in U]
    x = [jnp.where(eye, 1.0, 0.0) - lb[u] for u in U]
    lbb = [y.astype(BF16) for y in lb]
    p = [_dot(y, y) for y in lbb]
    lkv = [_dot(lk[u], vr[u]).astype(BF16) for u in U]
    n_iter = int(np.log2(L)) - 1
    for it in range(n_iter):
        pb = [y.astype(BF16) for y in p]
        x = [x[u] + _dot(x[u].astype(BF16), pb[u]) for u in U]
        if it + 1 < n_iter:
            p = [_dot(y, y) for y in pb]
    xb = [y.astype(BF16) for y in x]
    a12 = [(-_dot(xb[u], jnp.concatenate([kap[u], lkv[u]], axis=1))).astype(BF16) for u in U]
    btg = [(bt32[u] * g_end[u]).astype(BF16) for u in U]
    ktg = [(kt32[u] * g_end[u]).astype(BF16) for u in U]
    mn = [_dot_tn(btg[u], a12[u]) for u in U]
    ha = [_dot(hb[u], a12[u]) for u in U]
    m = [jnp.where(eye, jnp.broadcast_to(g_end[u], eye.shape), 0.0) + mn[u][:, :n2] for u in U]
    nn = [mn[u][:, n2:] + _dot_tn(ktg[u], vr[u]) for u in U]
    y1 = [rt32[u] + ha[u][:, :n2] for u in U]
    y2 = [ha[u][:, n2:] + _dot(hk[u], vr[u]) for u in U]
    return [(m[u].astype(BF16), nn[u], y1[u].astype(BF16), y2[u]) for u in U]


def _rwkv_apply(st, local):
    m, nn, y1, y2 = local
    L = RWKV_CHUNK
    stb = st.astype(BF16)
    ysm = _dot(y1, stb) + y2
    return ysm[:L, :] + ysm[L:, :], _dot(m, stb) + nn


def _rwkv_kernel(*refs, t_len, has_s0):
    (zr_ref, zk_ref, zv_ref, zs_ref, mur_ref, muk_ref, muv_ref, mus_ref, w0_ref, w2_ref, a0_ref, a2_ref,
     g2_ref, kkg_ref, ka_ref, rk_ref, ln_ref) = refs[:17]
    pos = 17
    if has_s0:
        s0f_ref, s0b_ref = refs[17:19]
        pos = 19
    o_ref, sf_ref, sb_ref = refs[pos:pos + 3]
    r_s, v_s, kk_s, bonus_s, g_s, lw_s, kh_s, bb_s, y_s = refs[pos + 3:]
    L = RWKV_CHUNK
    n_chunks = t_len // L
    lo = _low_half_mask((t_len, LANES))

    r = _centred_shift(zr_ref[...], mur_ref)
    k = _centred_shift(zk_ref[...], muk_ref)
    v = _centred_shift(zv_ref[...], muv_ref)
    s = _centred_shift(zs_ref[...], mus_ref)
    tw = jnp.tanh(s[:, 0:LANES])
    ad = s[:, LANES:2 * LANES]
    g_s[...] = _dot3(jax.nn.sigmoid(s[:, 2 * LANES:]), g2_ref[...])
    kk = k * kkg_ref[...]
    kk = kk / jnp.maximum(jnp.sqrt(_seg_sum(kk * kk, lo)), 1e-12)
    kh_sum = jnp.zeros_like(k)
    for d in range(2):
        keep = lo if d == 0 else jnp.logical_not(lo)
        w_pre = w0_ref[d:d + 1, :] + _dot3(jnp.where(keep, tw, 0.0), w2_ref[...])
        neg = -w_pre
        softplus = jnp.maximum(neg, 0.0) + jnp.log1p(jnp.exp(-jnp.abs(neg)))
        lw_s[d] = -jnp.exp(-softplus - 0.5)
        a = jax.nn.sigmoid(a0_ref[d:d + 1, :] + _dot3(jnp.where(keep, ad, 0.0), a2_ref[...]))
        kh = k * (1.0 + (a - 1.0) * ka_ref[...])
        kh_s[d] = kh
        bb_s[d] = kk * a
        kh_sum = kh_sum + kh
    bonus_s[...] = _seg_sum(r * kh_sum * rk_ref[...], lo) * v
    r_s[...] = r
    v_s[...] = v
    kk_s[...] = kk

    ri = lax.broadcasted_iota(I32, (2 * L, 2 * L), 0)
    ci = lax.broadcasted_iota(I32, (2 * L, 2 * L), 1)
    tt, ss = ri & (L - 1), ci & (L - 1)
    eye = ri == ci
    ti = lax.broadcasted_iota(I32, (L, L), 0)
    si = lax.broadcasted_iota(I32, (L, L), 1)
    lo_c = _low_half_mask((L, LANES))
    consts = ((jnp.where(si <= ti, 1.0, 0.0).astype(BF16), tt > ss, tt >= ss, eye, lo_c),
              (jnp.where(si >= ti, 1.0, 0.0).astype(BF16), tt < ss, tt <= ss, eye, lo_c))

    if has_s0:
        sf_ref[...] = s0f_ref[...]
        sb_ref[...] = s0b_ref[...]
    else:
        sf_ref[...] = jnp.zeros((2 * L, 2 * L), F32)
        sb_ref[...] = jnp.zeros((2 * L, 2 * L), F32)

    unroll = max(u for u in range(1, RWKV_UNROLL + 1) if n_chunks % u == 0)

    def group(i, carry):
        units = []
        for u in range(unroll):
            c = i * unroll + u
            starts = (c * L, (n_chunks - 1 - c) * L)
            if not isinstance(i, int):
                starts = tuple(pl.multiple_of(s, L) for s in starts)
            units += [(0, starts[0]), (1, starts[1])]
        local = _rwkv_chunks(units, r_s, v_s, kk_s, lw_s, kh_s, bb_s, consts)
        states = [sf_ref[...], sb_ref[...]]
        for (d, start), loc in zip(units, local):
            y_d, states[d] = _rwkv_apply(states[d], loc)
            y_s[d, pl.ds(start, L), :] = y_d
        sf_ref[...] = states[0]
        sb_ref[...] = states[1]
        return carry

    if n_chunks == unroll:
        group(0, 0)
    else:
        lax.fori_loop(0, n_chunks // unroll, group, 0)

    y = y_s[0] + y_s[1]
    inv_n = 1.0 / RWKV_HEAD
    mu = _seg_sum(y, lo) * inv_n
    yc = y - mu
    var = _seg_sum(yc * yc, lo) * inv_n
    yn = yc * lax.rsqrt(var + GN_EPS) * ln_ref[0:1, :] + ln_ref[1:2, :]
    o_ref[...] = ((yn + bonus_s[...]) * g_s[...]).astype(o_ref.dtype)


def _rwkv(z, out, par, nb, t_len, row0, s0=None):
    c = par['w0'].shape[1]
    pairs = c // LANES
    assert t_len % RWKV_CHUNK == 0 and row0 % t_len == 0 and (3 * c) % (4 * LANES) == 0
    rb0 = row0 // t_len
    cb = c // LANES
    sb = 3 * c // (4 * LANES)
    zmap = lambda off: (lambda b, j: (rb0 + b, off + j))
    cmap = lambda off: (lambda b, j: (0, off + j))
    in_specs = [pl.BlockSpec((t_len, LANES), zmap(0)), pl.BlockSpec((t_len, LANES), zmap(cb)),
                pl.BlockSpec((t_len, LANES), zmap(2 * cb)),
                pl.BlockSpec((t_len, 4 * LANES), lambda b, j: (rb0 + b, sb)),
                pl.BlockSpec((2, LANES), cmap(0)), pl.BlockSpec((2, LANES), cmap(cb)),
                pl.BlockSpec((2, LANES), cmap(2 * cb)), pl.BlockSpec((2, 4 * LANES), lambda b, j: (0, sb)),
                pl.BlockSpec((2, LANES), cmap(0)), pl.BlockSpec((LANES, LANES), cmap(0)),
                pl.BlockSpec((2, LANES), cmap(0)), pl.BlockSpec((LANES, LANES), cmap(0)),
                pl.BlockSpec((2 * LANES, LANES), cmap(0)),
                pl.BlockSpec((1, LANES), cmap(0)), pl.BlockSpec((1, LANES), cmap(0)),
                pl.BlockSpec((1, LANES), cmap(0)), pl.BlockSpec((2, LANES), cmap(0))]
    args = [z, z, z, z, par['mu'], par['mu'], par['mu'], par['mu'], par['w0'], par['w2'], par['a0'], par['a2'],
            par['g2'], par['k_k'], par['k_a'], par['r_k'], par['ln']]
    smap = lambda b, j: (b, j, 0, 0)
    if s0 is not None:
        in_specs += [pl.BlockSpec((None, None, LANES, LANES), smap)] * 2
        args += [s0[0], s0[1]]
    in_specs.append(pl.BlockSpec(memory_space=pl.ANY))
    args.append(out)
    st_shape = jax.ShapeDtypeStruct((nb, pairs, LANES, LANES), F32)
    tl = pltpu.VMEM((t_len, LANES), F32)
    tl2 = pltpu.VMEM((2, t_len, LANES), F32)

    def kern(*refs):
        n_in = len(args)
        _rwkv_kernel(*refs[:n_in - 1], *refs[n_in:], t_len=t_len, has_s0=s0 is not None)

    return pl.pallas_call(
        kern,
        grid=(nb, pairs),
        in_specs=in_specs,
        out_specs=[pl.BlockSpec((t_len, LANES), zmap(0)),
                   pl.BlockSpec((None, None, LANES, LANES), smap),
                   pl.BlockSpec((None, None, LANES, LANES), smap)],
        out_shape=[jax.ShapeDtypeStruct(out.shape, out.dtype), st_shape, st_shape],
        scratch_shapes=[tl, tl, tl, tl, tl, tl2, tl2, tl2, tl2],
        input_output_aliases={len(args) - 1: 0},
        compiler_params=_cparams(("arbitrary", "arbitrary")),
    )(*args)


def _rwkv_params(p):
    c = p['w0'].shape[1]
    return dict(mu=p['mu'], w0=p['w0'], w2=p['w2'].reshape(-1, c), a0=p['a0'], a2=p['a2'].reshape(-1, c),
                g2=p['g2'], k_k=p['k_k'].reshape(1, c), k_a=p['k_a'].reshape(1, c), r_k=p['r_k'].reshape(1, c),
                ln=p['ln'])


def _states_to_blockdiag(s):
    b, h, n, _ = s.shape
    st = jnp.swapaxes(s, 2, 3).reshape(b, h // 2, 2, n, n)
    z = jnp.zeros((b, h // 2, n, n), s.dtype)
    top = jnp.concatenate([st[:, :, 0], z], axis=-1)
    bot = jnp.concatenate([z, st[:, :, 1]], axis=-1)
    return jnp.concatenate([top, bot], axis=-2)


def _blockdiag_to_states(bd):
    b, hp, _, _ = bd.shape
    n = RWKV_HEAD
    st = jnp.stack([bd[:, :, :n, :n], bd[:, :, n:, n:]], axis=2)
    return jnp.swapaxes(st, 3, 4).reshape(b, 2 * hp, n, n)


def _ffn_pre_kernel(y_ref, g_ref, sh_ref, sc_ref, wh_ref, wl_ref, b_ref, hp_ref, route_ref, *, n_exp, n_grp):
    y = y_ref[...]
    ms = jnp.mean(y * y, axis=-1, keepdims=True)
    h = y * lax.rsqrt(ms + NORM_EPS) * g_ref[...]
    h = h * (1.0 + sc_ref[...]) + sh_ref[...]
    hp_ref[...] = h
    hh, hl = _split(h)
    wh = wh_ref[...]
    lg = _dot(hh, wh) + _dot(hl, wh) + _dot(hh, wl_ref[...]) + b_ref[...]
    lane = lax.broadcasted_iota(I32, lg.shape, 1)
    lane_f = lane.astype(F32)
    big = 1e9
    is_g = (lane >= n_exp) & (lane < n_exp + n_grp)
    gl = jnp.where(is_g, lg, NEG_BIG)
    gmax = jnp.max(gl, axis=-1, keepdims=True)
    gidx = jnp.min(jnp.where(is_g & (gl == gmax), lane_f, big), axis=-1, keepdims=True) - n_exp
    gprob = 1.0 / jnp.sum(jnp.where(is_g, jnp.exp(gl - gmax), 0.0), axis=-1, keepdims=True)
    gstart = gidx * MOE_GROUP_EXPERTS
    in_grp = (lane_f >= gstart) & (lane_f < gstart + MOE_GROUP_EXPERTS)
    el = jnp.where(in_grp, lg, NEG_BIG)
    e1v = jnp.max(el, axis=-1, keepdims=True)
    e1i = jnp.min(jnp.where(in_grp & (el == e1v), lane_f, big), axis=-1, keepdims=True)
    rest = in_grp & (lane_f != e1i)
    el2 = jnp.where(rest, lg, NEG_BIG)
    e2v = jnp.max(el2, axis=-1, keepdims=True)
    e2i = jnp.min(jnp.where(rest & (el2 == e2v), lane_f, big), axis=-1, keepdims=True)
    t = jnp.exp(e2v - e1v)
    den = 1.0 + t
    w1 = gprob / den
    w2 = gprob * t / den
    route_ref[...] = jnp.where(lane == 0, e1i, jnp.where(lane == 1, e2i,
                               jnp.where(lane == 2, w1, jnp.where(lane == 3, w2, 0.0))))


def _ffn_pre(y, gain, m6, rows, wr_hi, wr_lo, br, n_exp, n_grp):
    n, d = y.shape
    tm = _pick(np.gcd(rows.n_p, rows.ts), (256, 128, 64, 32, 16, 8))
    grp = rows.group_of_block(tm)
    const = lambda i: (0, 0)
    return pl.pallas_call(
        functools.partial(_ffn_pre_kernel, n_exp=n_exp, n_grp=n_grp),
        grid=(n // tm,),
        in_specs=[pl.BlockSpec((tm, d), lambda i: (i, 0)),
                  pl.BlockSpec((1, d), const),
                  pl.BlockSpec((None, 1, d), lambda i: (grp(i) * 6 + 3, 0, 0)),
                  pl.BlockSpec((None, 1, d), lambda i: (grp(i) * 6 + 4, 0, 0)),
                  pl.BlockSpec((d, LANES), const), pl.BlockSpec((d, LANES), const),
                  pl.BlockSpec((1, LANES), const)],
        out_specs=[pl.BlockSpec((tm, d), lambda i: (i, 0)), pl.BlockSpec((tm, LANES), lambda i: (i, 0))],
        out_shape=[jax.ShapeDtypeStruct((n, d), F32), jax.ShapeDtypeStruct((n, LANES), F32)],
        compiler_params=_cparams(("arbitrary",)),
    )(y, gain.reshape(1, d), m6, m6, wr_hi, wr_lo, br)


def _dispatch_kernel(dest_ref, hp_ref, _zeros_ref, xs_ref, sem, *, tm):
    base = pl.program_id(0) * tm

    def row_copy(src_row, dst_row):
        return pltpu.make_async_copy(hp_ref.at[pl.ds(src_row, 1), :], xs_ref.at[pl.ds(dst_row, 1), :], sem)

    def start(r, carry):
        a = MOE_TOPK * (base + r)
        for k in range(MOE_TOPK):
            row_copy(r, dest_ref[a + k]).start()
        return carry

    def wait(r, carry):
        for k in range(MOE_TOPK):
            row_copy(0, 0).wait()
        return carry

    lax.fori_loop(0, tm, start, 0, unroll=DMA_LOOP_UNROLL)
    lax.fori_loop(0, tm, wait, 0, unroll=DMA_LOOP_UNROLL)


def _moe_dispatch(dest, hp, n_slots, slots_init=None):
    n = hp.shape[0]
    if slots_init is None:
        slots_init = jnp.zeros((n_slots, hp.shape[1]), hp.dtype)
    assert slots_init.shape == (n_slots, hp.shape[1]) and slots_init.dtype == hp.dtype
    tm = _pick(n, (256, 128, 64, 32, 16, 8))
    any_spec = pl.BlockSpec(memory_space=pl.ANY)
    return pl.pallas_call(
        functools.partial(_dispatch_kernel, tm=tm),
        grid_spec=pltpu.PrefetchScalarGridSpec(
            num_scalar_prefetch=1, grid=(n // tm,),
            in_specs=[pl.BlockSpec((tm, hp.shape[1]), lambda i, dest: (i, 0)), any_spec], out_specs=any_spec,
            scratch_shapes=[pltpu.SemaphoreType.DMA]),
        out_shape=jax.ShapeDtypeStruct((n_slots, hp.shape[1]), hp.dtype),
        input_output_aliases={2: 0},
        compiler_params=_cparams(("arbitrary",)),
    )(dest, hp, slots_init)


def _expert_kernel(blk_e_ref, nact_ref, x_ref, wg_ref, wu_ref, wd_ref, o_ref, xb_ref):
    i = pl.program_id(0)
    j = pl.program_id(1)

    @pl.when(i < nact_ref[0])
    def _():
        @pl.when(j == 0)
        def _():
            xb_ref[...] = x_ref[...].astype(BF16)

        xb = xb_ref[...]
        gate = _dot(xb, wg_ref[...].astype(BF16))
        hid = (gate * jax.nn.sigmoid(gate) * _dot(xb, wu_ref[...].astype(BF16))).astype(BF16)

        d = o_ref.shape[1]
        cw = _pick(d, (1024, 512, 256, 128))
        for c in range(d // cw):
            cols = slice(c * cw, (c + 1) * cw)
            part = _dot(hid, wd_ref[:, cols].astype(BF16))

            @pl.when(j == 0)
            def _():
                o_ref[:, cols] = part

            @pl.when(j > 0)
            def _():
                o_ref[:, cols] += part


def _moe_experts(xs, blk_e, nact, w_gate, w_up, w_down, layer):
    n_slots, d = xs.shape
    _, n_exp, _, hid = w_gate.shape
    bm = MOE_BLOCK_ROWS
    hb = _pick(hid, (MOE_HIDDEN_BLOCK, 128))
    nh = hid // hb

    def active(i, j, blk_e, nact):
        ib = jnp.minimum(i, nact[0] - 1)
        jb = jnp.where(i < nact[0], j, nh - 1)
        return ib, jb

    def xmap(i, j, blk_e, nact):
        return active(i, j, blk_e, nact)[0], 0

    def gmap(i, j, blk_e, nact):
        ib, jb = active(i, j, blk_e, nact)
        return layer, blk_e[ib], 0, jb

    def dmap(i, j, blk_e, nact):
        ib, jb = active(i, j, blk_e, nact)
        return layer, blk_e[ib], jb, 0

    return pl.pallas_call(
        _expert_kernel,
        grid_spec=pltpu.PrefetchScalarGridSpec(
            num_scalar_prefetch=2, grid=(n_slots // bm, nh),
            in_specs=[pl.BlockSpec((bm, d), xmap),
                      pl.BlockSpec((None, None, d, hb), gmap), pl.BlockSpec((None, None, d, hb), gmap),
                      pl.BlockSpec((None, None, hb, d), dmap)],
            out_specs=pl.BlockSpec((bm, d), xmap),
            scratch_shapes=[pltpu.VMEM((bm, d), BF16)]),
        out_shape=jax.ShapeDtypeStruct((n_slots, d), F32),
        input_output_aliases={2: 0},
        compiler_params=_cparams(("arbitrary", "arbitrary")),
    )(blk_e, nact, xs, w_gate, w_up, w_down)


def _combine_kernel(dest_ref, y_ref, gate_ref, w_ref, ys_ref, o_ref, buf, sem, *, tm, n_steps):
    i = pl.program_id(0)
    slot = lax.rem(i, 2)

    def row_copy(src_row, s, k, r):
        return pltpu.make_async_copy(ys_ref.at[pl.ds(src_row, 1), :], buf.at[s, k, pl.ds(r, 1), :], sem.at[s])

    def issue(step, s):
        def body(r, carry):
            a = 2 * (step * tm + r)
            row_copy(dest_ref[a], s, 0, r).start()
            row_copy(dest_ref[a + 1], s, 1, r).start()
            return carry
        lax.fori_loop(0, tm, body, 0, unroll=DMA_LOOP_UNROLL)

    @pl.when(i == 0)
    def _():
        issue(0, 0)

    @pl.when(i + 1 < n_steps)
    def _():
        issue(i + 1, 1 - slot)

    def wait(r, carry):
        row_copy(0, slot, 0, 0).wait()
        row_copy(0, slot, 1, 0).wait()
        return carry
    lax.fori_loop(0, tm, wait, 0, unroll=DMA_LOOP_UNROLL)

    w0 = w_ref[:, 0:1]
    w1 = w_ref[:, 1:2]
    o_ref[...] = y_ref[...] + gate_ref[...] * (w0 * buf[slot, 0] + w1 * buf[slot, 1])


def _moe_combine(dest, y, m6, rows, wts, ys):
    n, d = y.shape
    tm = _pick(np.gcd(rows.n_p, rows.ts), (128, 64, 32, 16, 8))
    grp = rows.group_of_block(tm)
    n_steps = n // tm
    return pl.pallas_call(
        functools.partial(_combine_kernel, tm=tm, n_steps=n_steps),
        grid_spec=pltpu.PrefetchScalarGridSpec(
            num_scalar_prefetch=1, grid=(n_steps,),
            in_specs=[pl.BlockSpec((tm, d), lambda i, dest: (i, 0)),
                      pl.BlockSpec((None, 1, d), lambda i, dest: (grp(i) * 6 + 5, 0, 0)),
                      pl.BlockSpec((tm, MOE_TOPK), lambda i, dest: (i, 0)),
                      pl.BlockSpec(memory_space=pl.ANY)],
            out_specs=pl.BlockSpec((tm, d), lambda i, dest: (i, 0)),
            scratch_shapes=[pltpu.VMEM((2, MOE_TOPK, tm, d), F32), pltpu.SemaphoreType.DMA((2,))]),
        out_shape=jax.ShapeDtypeStruct((n, d), F32),
        compiler_params=_cparams(("arbitrary",)),
    )(dest, y, m6, wts, ys)


def _moe_layer(y, gain, m6, rows, w_group, b_group, w_expert, b_expert, w_gate, w_up, w_down, layer, slots_init):
    n, d = y.shape
    n_exp = w_expert.shape[1]
    n_grp = w_group.shape[1]
    assert n_exp + n_grp <= LANES and n_exp == n_grp * MOE_GROUP_EXPERTS
    pad = jnp.zeros((d, LANES - n_exp - n_grp), F32)
    wr = jnp.concatenate([w_expert, w_group, pad], axis=1)
    wr_hi, wr_lo = _split(wr)
    br = jnp.concatenate([b_expert, b_group, jnp.zeros((LANES - n_exp - n_grp,), F32)]).reshape(1, LANES)
    hp, route = _ffn_pre(y, gain, m6, rows, wr_hi, wr_lo, br, n_exp, n_grp)

    bm = MOE_BLOCK_ROWS
    n_assign = n * MOE_TOPK
    flat_e = route[:, :MOE_TOPK].astype(I32).reshape(n_assign)
    wts = route[:, MOE_TOPK:2 * MOE_TOPK]
    onehot = (flat_e[:, None] == jnp.arange(n_exp, dtype=I32)[None, :]).astype(I32)
    csum = jnp.cumsum(onehot, axis=0)
    counts = csum[-1]
    padded = (counts + bm - 1) // bm * bm
    pad_end = jnp.cumsum(padded)
    dest = jnp.sum(onehot * (csum - 1 + (pad_end - padded)[None, :]), axis=1)
    n_blocks = (n_assign + n_exp * (bm - 1) + bm - 1) // bm
    blk_start = jnp.arange(n_blocks, dtype=I32) * bm
    blk_e = jnp.minimum(jnp.sum((pad_end[None, :] <= blk_start[:, None]).astype(I32), axis=1), n_exp - 1)
    nact = (pad_end[-1:] // bm).astype(I32)

    xs = _moe_dispatch(dest, hp, n_blocks * bm, slots_init)
    ys = _moe_experts(xs, blk_e, nact, w_gate, w_up, w_down, layer)
    return _moe_combine(dest, y, m6, rows, wts, ys), ys


def _rope_rows(t_len, rot_dim):
    rows = t_len // GRID_W
    row = jnp.repeat(jnp.arange(rows, dtype=F32), GRID_W)
    col = jnp.tile(jnp.arange(GRID_W, dtype=F32), rows)
    quarter = rot_dim // 4
    inv = ROPE_THETA ** (-jnp.arange(quarter, dtype=F32) / quarter)
    ang = jnp.concatenate([row[:, None] * inv, col[:, None] * inv], axis=-1)
    cos, sin = jnp.cos(ang), jnp.sin(ang)
    reps = LANES // rot_dim
    return (jnp.tile(jnp.concatenate([cos, cos], axis=-1), (1, reps)),
            jnp.tile(jnp.concatenate([-sin, sin], axis=-1), (1, reps)))


def _token_tables(rows, rot_dim):
    cos_s, sin_s = _rope_rows(rows.ts, rot_dim)
    cos = jnp.concatenate([jnp.ones((rows.n_p, LANES), F32)] + [cos_s] * rows.bs, axis=0)
    sin = jnp.concatenate([jnp.zeros((rows.n_p, LANES), F32)] + [sin_s] * rows.bs, axis=0)
    return cos, sin


def _kv_order(rows, tok, ctx):
    parts = []
    for b in range(rows.bs):
        parts += [ctx[b], tok[rows.n_p + b * rows.ts:rows.n_p + (b + 1) * rows.ts]]
    parts.append(tok[:rows.n_p])
    return jnp.concatenate(parts, axis=0)


def _pad_rows8(rows_list):
    a = jnp.stack(rows_list, axis=0)
    return jnp.concatenate([a, jnp.zeros((8 - a.shape[0], a.shape[1]), a.dtype)], axis=0)


def _ab_layer(y, h, m6, rows, past, p, ctx):
    n, d = y.shape
    q_rank = p['q_norm'].shape[0]
    kv_rank = p['kv_norm'].shape[0]
    heads = p['w_uq'].shape[1] // (MLA_NOPE + MLA_ROPE)
    mla_in = q_rank + kv_rank + MLA_ROPE
    half = MLA_ROPE // 2
    swap = np.concatenate([np.arange(half, MLA_ROPE), np.arange(0, half)])

    w_in = p['w_in']
    w_kr = w_in[:, q_rank + kv_rank:mla_in]
    w_mla = jnp.concatenate([w_in[:, :q_rank + kv_rank], w_kr, w_kr, w_kr[:, swap], w_kr[:, swap]], axis=1).astype(BF16)
    zm = _matmul([h], [w_mla])
    z_rwkv = _matmul([h], [w_in[:, mla_in:].astype(BF16)])

    cqn, ckvn, ckvn_b = _mla_mid(zm, p['q_norm'], p['kv_norm'], q_rank, kv_rank)
    w_uq = p['w_uq'].reshape(q_rank, heads, MLA_NOPE + MLA_ROPE)
    w_rope = w_uq[:, :, MLA_NOPE:]
    w_q = jnp.concatenate([w_uq[:, :, :MLA_NOPE].reshape(q_rank, -1), w_rope.reshape(q_rank, -1),
                           w_rope[:, :, swap].reshape(q_rank, -1)], axis=1).astype(BF16)
    zq = _matmul([cqn], [w_q])
    cos_t, sin_t = _token_tables(rows, MLA_ROPE)

    def gains(g):
        g_r = g[MLA_NOPE:]
        return _pad_rows8([g[:MLA_NOPE], jnp.concatenate([g_r, g_r]), jnp.concatenate([g_r[swap], g_r[swap]])])

    scale = float(MLA_NOPE + MLA_ROPE) ** -0.5
    qn, qr = _mla_q_prep(zq, gains(p['qk_norm'][0]), cos_t, sin_t, heads, scale)

    ctx_ckv, ctx_kr, s_f0, s_b0 = ctx
    w_ukv = p['w_ukv'].reshape(kv_rank, heads, MLA_NOPE + MLA_V)
    w_kv = jnp.concatenate([w_ukv[:, :, :MLA_NOPE].reshape(kv_rank, -1), w_ukv[:, :, MLA_NOPE:].reshape(kv_rank, -1)],
                           axis=1).astype(BF16)
    zkv = _matmul([_kv_order(rows, ckvn_b, ctx_ckv.astype(BF16))], [w_kv])
    kr_off = q_rank + kv_rank
    ctx_kr2 = jnp.concatenate([ctx_kr, ctx_kr], axis=-1)
    kr2 = _kv_order(rows, zm[:, kr_off:kr_off + LANES], ctx_kr2)
    krs2 = _kv_order(rows, zm[:, kr_off + LANES:kr_off + 2 * LANES], ctx_kr2)
    one_c = jnp.ones((rows.bs, past, LANES), F32)
    cos_kv = _kv_order(rows, cos_t, one_c)
    sin_kv = _kv_order(rows, sin_t, 0.0 * one_c)
    kn, krh, vv = _mla_k_prep(zkv, kr2, krs2, gains(p['qk_norm'][1]), cos_kv, sin_kv, heads)

    o_mla = jnp.zeros((n, heads * MLA_V), BF16)
    o_mla = _mla_attn(qn, qr, kn, krh, vv, o_mla, heads, rows.bp, rows.tp, rows.tp, 0, rows.bs * (past + rows.ts))
    o_mla = _mla_attn(qn, qr, kn, krh, vv, o_mla, heads, rows.bs, rows.ts, past + rows.ts, rows.n_p, 0)

    par = _rwkv_params(p)
    c = par['w0'].shape[1]
    o_rwkv = jnp.zeros((n, c), BF16)
    o_rwkv, sf, sb = _rwkv(z_rwkv, o_rwkv, par, rows.bp, rows.tp, 0)
    s0 = (_states_to_blockdiag(s_f0), _states_to_blockdiag(s_b0))
    o_rwkv, _, _ = _rwkv(z_rwkv, o_rwkv, par, rows.bs, rows.ts, rows.n_p, s0)

    hm = heads * MLA_V
    w_out = p['w_out'].astype(BF16)
    y = _matmul([o_mla, o_rwkv], [w_out[:hm], w_out[hm:]], res=y, m6=m6, gate_idx=2, rows=rows)
    new_ckv = ckvn[:rows.n_p].reshape(rows.bp, rows.tp, kv_rank)
    new_kr = zm[:rows.n_p, kr_off:kr_off + MLA_ROPE].reshape(rows.bp, rows.tp, MLA_ROPE)
    return y, (new_ckv, new_kr, _blockdiag_to_states(sf), _blockdiag_to_states(sb))


def _gqa_layer(y, h, m6, rows, past, p, ctx):
    n, d = y.shape
    heads = p['w_out'].shape[0] // ATTN_HEAD
    kv_heads = heads // ATTN_GROUP
    nq, nk = heads * ATTN_HEAD, kv_heads * ATTN_HEAD
    z = _matmul([h], [p['w_in'].astype(BF16)])
    cos_t, sin_t = _token_tables(rows, ATTN_HEAD)
    g2 = _pad_rows8([p['qk_norm'][0], p['qk_norm'][1]])
    q, k_keep, k_rot, v_b = _gqa_prep(z, g2, cos_t, sin_t, heads, kv_heads, float(ATTN_HEAD) ** -0.5)
    ctx_k, ctx_v = ctx
    k_all = _kv_order(rows, k_rot, ctx_k.reshape(rows.bs, past, nk).astype(BF16))
    v_all = _kv_order(rows, v_b, ctx_v.reshape(rows.bs, past, nk).astype(BF16))
    o = jnp.zeros((n, nq), BF16)
    o = _gqa_attn(q, k_all, v_all, o, kv_heads, rows.bp, rows.tp, rows.tp, 0, rows.bs * (past + rows.ts))
    o = _gqa_attn(q, k_all, v_all, o, kv_heads, rows.bs, rows.ts, past + rows.ts, rows.n_p, 0)
    y = _matmul([o], [p['w_out'].astype(BF16)], res=y, m6=m6, gate_idx=2, rows=rows)
    new_k = k_keep[:rows.n_p].reshape(rows.bp, rows.tp, kv_heads, ATTN_HEAD)
    new_v = z[:rows.n_p, nq + nk:].reshape(rows.bp, rows.tp, kv_heads, ATTN_HEAD)
    return y, (new_k, new_v)


def kernel(x_prompt, x_sample, c, c_ctx, cache_mla_ckv, cache_mla_krope, state_rwkv_fwd, state_rwkv_bwd, cache_attn_k, cache_attn_v, mod_w, mod_b, norm_mix, norm_ffn, ab_w_in, ab_w_out, mla_q_norm, mla_w_uq, mla_kv_norm, mla_w_ukv, mla_qk_norm, rwkv_mu, rwkv_w0, rwkv_w2, rwkv_a0, rwkv_a2, rwkv_g2, rwkv_k_k, rwkv_k_a, rwkv_r_k, rwkv_ln, gqa_w_in, gqa_qk_norm, gqa_w_out, moe_w_group, moe_b_group, moe_w_expert, moe_b_expert, moe_w_gate, moe_w_up, moe_w_down):
    bp, tp, d = x_prompt.shape
    bs, ts, _ = x_sample.shape
    past = cache_mla_ckv.shape[2]
    depth = mod_w.shape[0]
    rows = _Rows(bp, tp, bs, ts)
    assert bs + 1 <= 8

    cond8 = jnp.concatenate([c_ctx[None, :], c, jnp.zeros((8 - 1 - bs, d), F32)], axis=0)
    m_all = _adaln_all(cond8, mod_w, mod_b)
    y = jnp.concatenate([x_prompt.reshape(bp * tp, d), x_sample.reshape(bs * ts, d)], axis=0)

    ab_out, gqa_out = [], []
    slots = None
    for layer in range(depth):
        m6 = m_all[layer].reshape(8 * 6, 1, d)
        h = _norm_mod(y, norm_mix[layer], m6, rows, 0, 1)
        i = layer // 2
        if layer % 2 == 0:
            p = dict(w_in=ab_w_in[i], w_out=ab_w_out[i], q_norm=mla_q_norm[i], w_uq=mla_w_uq[i],
                     kv_norm=mla_kv_norm[i], w_ukv=mla_w_ukv[i], qk_norm=mla_qk_norm[i], mu=rwkv_mu[i],
                     w0=rwkv_w0[i], w2=rwkv_w2[i], a0=rwkv_a0[i], a2=rwkv_a2[i], g2=rwkv_g2[i],
                     k_k=rwkv_k_k[i], k_a=rwkv_k_a[i], r_k=rwkv_r_k[i], ln=rwkv_ln[i])
            y, new = _ab_layer(y, h, m6, rows, past, p, (cache_mla_ckv[:, i], cache_mla_krope[:, i],
                                                          state_rwkv_fwd[:, i], state_rwkv_bwd[:, i]))
            ab_out.append(new)
        else:
            p = dict(w_in=gqa_w_in[i], qk_norm=gqa_qk_norm[i], w_out=gqa_w_out[i])
            y, new = _gqa_layer(y, h, m6, rows, past, p, (cache_attn_k[:, i], cache_attn_v[:, i]))
            gqa_out.append(new)
        y, slots = _moe_layer(y, norm_ffn[layer], m6, rows, moe_w_group[layer], moe_b_group[layer],
                              moe_w_expert[layer], moe_b_expert[layer], moe_w_gate, moe_w_up, moe_w_down, layer, slots)

    yp = y[:rows.n_p].reshape(bp, tp, d)
    ys = y[rows.n_p:].reshape(bs, ts, d)
    stack = lambda outs, k: jnp.stack([o[k] for o in outs], axis=1)
    return (yp, ys, stack(ab_out, 0), stack(ab_out, 1), stack(ab_out, 2), stack(ab_out, 3),
            stack(gqa_out, 0), stack(gqa_out, 1))
```
